```python
import jax, jax.numpy as jnp
from jax import lax
import numpy as np

D_MODEL = 1024
BATCH = 8
SEQ = 8192
DEPTH = 1
DEC_BATCH = 8
DEC_SEQ = 16
PAST_LEN = 1024

CHUNK = 64
D_MIX = D_MODEL
D_ATT = D_MIX // 2
HEAD_DIM = 64
N_HEADS = D_ATT // HEAD_DIM
D_CONV = D_MIX - D_ATT
CONV_W = 31
Q_BLOCK = 128
EPS = 1e-6
LN_EPS = 1e-5
D_IN = 4 * D_ATT + 3 * D_CONV
SPLITS = (D_ATT, 2 * D_ATT, 3 * D_ATT, 4 * D_ATT, 4 * D_ATT + D_CONV, 4 * D_ATT + 2 * D_CONV)

kernel_name = "stickbreak_conformer_hymba_stream_step"


def rmsnorm(x, w):
    xf = x.astype(jnp.float32)
    r = lax.rsqrt(jnp.mean(xf * xf, axis=-1, keepdims=True) + EPS)
    return (xf * r * w.astype(jnp.float32)).astype(x.dtype)


def layernorm(x, w, b):
    xf = x.astype(jnp.float32)
    mu = jnp.mean(xf, axis=-1, keepdims=True)
    xc = xf - mu
    var = jnp.mean(xc * xc, axis=-1, keepdims=True)
    y = xc * lax.rsqrt(var + LN_EPS) * w.astype(jnp.float32) + b.astype(jnp.float32)
    return y.astype(x.dtype)


def to_heads(t):
    b, s, _ = t.shape
    return t.reshape(b, s, N_HEADS, HEAD_DIM).transpose(0, 2, 1, 3)


def from_heads(t):
    b, h, s, d = t.shape
    return t.transpose(0, 2, 1, 3).reshape(b, s, h * d)


def sb_block(q, k, v, q_pos):
    z = jnp.einsum('bhqd,bhkd->bhqk', q.astype(jnp.float32), k.astype(jnp.float32)) * (HEAD_DIM ** -0.5)
    k_pos = jnp.arange(k.shape[2], dtype=jnp.int32)
    valid = k_pos[None, :] < q_pos[:, None]
    log_fail = jnp.where(valid, jax.nn.log_sigmoid(-z), 0.0)
    suffix = lax.cumsum(log_fail, axis=3, reverse=True) - log_fail
    a = jnp.where(valid, jnp.exp(jax.nn.log_sigmoid(z) + suffix), 0.0)
    return jnp.einsum('bhqk,bhkd->bhqd', a, v.astype(jnp.float32))


def sb_prompt(q, k, v):
    b, h, s, d = q.shape
    nb = s // Q_BLOCK
    qb = q.reshape(b, h, nb, Q_BLOCK, d).transpose(2, 0, 1, 3, 4)
    pos = jnp.arange(s, dtype=jnp.int32).reshape(nb, Q_BLOCK)
    out = lax.map(lambda a: sb_block(a[0], k, v, a[1]), (qb, pos))
    return out.transpose(1, 2, 0, 3, 4).reshape(b, h, s, d)


def conv_tail(u_ext, dw_w, dw_b, cn_w, cn_b, w_pw, b_pw):
    c = lax.conv_general_dilated(u_ext, dw_w[:, None, :].astype(u_ext.dtype), window_strides=(1,),
                                 padding='VALID', dimension_numbers=('NWC', 'WIO', 'NWC'),
                                 feature_group_count=D_CONV)
    c = c + dw_b
    c = jax.nn.silu(layernorm(c, cn_w, cn_b))
    return jnp.einsum('bsc,ce->bse', c, w_pw) + b_pw


def split_proj(h, w_in):
    proj = jnp.einsum('bsd,de->bse', h, w_in)
    q, k, v, g_a, glu_v, glu_g, g_c = jnp.split(proj, SPLITS, axis=-1)
    u = glu_v * jax.nn.sigmoid(glu_g)
    return to_heads(q), to_heads(k), to_heads(v), g_a, u, g_c


def merge(x, att, g_a, conv, g_c, w_out):
    mixed = jnp.concatenate([att.astype(x.dtype) * jax.nn.silu(g_a), conv * jax.nn.silu(g_c)], axis=-1)
    return x + jnp.einsum('bse,ed->bsd', mixed, w_out)


def setup_inputs(seed: int = 0) -> dict:
    key = jax.random.key(seed)
    ks = jax.random.split(key, 16)
    f32 = jnp.float32
    return {
        "x_prompt": jax.random.normal(ks[0], (BATCH, SEQ, D_MODEL), f32),
        "x_sample": jax.random.normal(ks[1], (DEC_BATCH, DEC_SEQ, D_MODEL), f32),
        "cache_k": jax.random.normal(ks[2], (DEPTH, DEC_BATCH, N_HEADS, PAST_LEN, HEAD_DIM), f32),
        "cache_v": jax.random.normal(ks[3], (DEPTH, DEC_BATCH, N_HEADS, PAST_LEN, HEAD_DIM), f32),
        "state_conv": 0.5 * jax.random.normal(ks[4], (DEPTH, DEC_BATCH, CONV_W - 1, D_CONV), f32),
        "norm_w": 1.0 + 0.02 * jax.random.normal(ks[5], (DEPTH, D_MODEL), f32),
        "w_in": jax.random.normal(ks[6], (DEPTH, D_MODEL, D_IN), f32) * D_MODEL ** -0.5,
        "dw_w": jax.random.normal(ks[7], (DEPTH, CONV_W, D_CONV), f32) * CONV_W ** -0.5,
        "dw_b": 0.02 * jax.random.normal(ks[8], (DEPTH, D_CONV), f32),
        "cn_w": 1.0 + 0.02 * jax.random.normal(ks[9], (DEPTH, D_CONV), f32),
        "cn_b": 0.02 * jax.random.normal(ks[10], (DEPTH, D_CONV), f32),
        "w_pw": jax.random.normal(ks[11], (DEPTH, D_CONV, D_CONV), f32) * D_CONV ** -0.5,
        "b_pw": 0.02 * jax.random.normal(ks[12], (DEPTH, D_CONV), f32),
        "w_out": jax.random.normal(ks[13], (DEPTH, D_MIX, D_MODEL), f32) * D_MIX ** -0.5,
        "final_norm_w": 1.0 + 0.02 * jax.random.normal(ks[14], (D_MODEL,), f32),
    }


def reference(x_prompt, x_sample, cache_k, cache_v, state_conv, norm_w, w_in, dw_w, dw_b,
              cn_w, cn_b, w_pw, b_pw, w_out, final_norm_w):
    n_new = x_sample.shape[1]
    assert n_new <= CHUNK
    past = cache_k.shape[3]
    xp, xs = x_prompt, x_sample
    kp_l, vp_l, cp_l, ks_l, vs_l, cs_l = [], [], [], [], [], []
    for l in range(DEPTH):
        h = rmsnorm(xp, norm_w[l])
        q, k, v, g_a, u, g_c = split_proj(h, w_in[l])
        att = from_heads(sb_prompt(q, k, v))
        u_ext = jnp.concatenate([jnp.zeros((u.shape[0], CONV_W - 1, D_CONV), u.dtype), u], axis=1)
        conv = conv_tail(u_ext, dw_w[l], dw_b[l], cn_w[l], cn_b[l], w_pw[l], b_pw[l])
        xp = merge(xp, att, g_a, conv, g_c, w_out[l])
        kp_l.append(k)
        vp_l.append(v)
        cp_l.append(u_ext[:, -(CONV_W - 1):])
        h = rmsnorm(xs, norm_w[l])
        q, k, v, g_a, u, g_c = split_proj(h, w_in[l])
        k_all = jnp.concatenate([cache_k[l].astype(k.dtype), k], axis=2)
        v_all = jnp.concatenate([cache_v[l].astype(v.dtype), v], axis=2)
        q_pos = past + jnp.arange(n_new, dtype=jnp.int32)
        att = from_heads(sb_block(q, k_all, v_all, q_pos))
        u_ext = jnp.concatenate([state_conv[l].astype(u.dtype), u], axis=1)
        conv = conv_tail(u_ext, dw_w[l], dw_b[l], cn_w[l], cn_b[l], w_pw[l], b_pw[l])
        xs = merge(xs, att, g_a, conv, g_c, w_out[l])
        ks_l.append(k)
        vs_l.append(v)
        cs_l.append(u_ext[:, -(CONV_W - 1):])
    y_prompt = rmsnorm(xp, final_norm_w)
    y_sample = rmsnorm(xs, final_norm_w)
    return (y_prompt, y_sample, jnp.stack(kp_l), jnp.stack(vp_l), jnp.stack(cp_l),
            jnp.stack(ks_l), jnp.stack(vs_l), jnp.stack(cs_l))
```

```python
import functools
import math

import jax
import jax.numpy as jnp
from jax import lax
from jax.experimental import pallas as pl
from jax.experimental.pallas import tpu as pltpu

D_MODEL = 1024
D_ATT = 512
D_CONV = 512
HEAD_DIM = 64
N_HEADS = D_ATT // HEAD_DIM
CONV_W = 31
HIST = 32
EPS = 1e-6
LN_EPS = 1e-5
D_IN = 4 * D_ATT + 3 * D_CONV
LANES = 128
LOG2E = 1.4426950408889634
VMEM_LIMIT = 56 * 1024 * 1024

BF16 = jnp.bfloat16
F32 = jnp.float32


def _sigmoid(x):
    return 1.0 / (1.0 + jnp.exp(-x))


def _silu(x):
    return x * _sigmoid(x)


def _in_proj_kernel(x_ref, nw_ref, w_ref, q_ref, k_ref, v_ref, ga_ref, u_ref, gc_ref, ko_ref, vo_ref):
    x = x_ref[0]
    r = lax.rsqrt(jnp.mean(x * x, axis=-1, keepdims=True) + EPS)
    h = (x * r * nw_ref[...]).astype(BF16)

    def proj(c):
        return jnp.dot(h, w_ref[:, c * D_ATT:(c + 1) * D_ATT], preferred_element_type=F32)

    q = proj(0)
    q_ref[0] = (q * (LOG2E * HEAD_DIM ** -0.5)).astype(BF16)
    k = proj(1)
    k_ref[0] = k.astype(BF16)
    for hd in range(N_HEADS):
        ko_ref[0, 0, hd] = k[:, hd * HEAD_DIM:(hd + 1) * HEAD_DIM]
    v = proj(2)
    v_ref[0] = v.astype(BF16)
    for hd in range(N_HEADS):
        vo_ref[0, 0, hd] = v[:, hd * HEAD_DIM:(hd + 1) * HEAD_DIM]
    ga_ref[0] = _silu(proj(3)).astype(BF16)
    u_ref[0] = proj(4) * _sigmoid(proj(5))
    gc_ref[0] = _silu(proj(6)).astype(BF16)


def _in_proj(x, norm_w, w_in_bf, ts):
    b, s, _ = x.shape
    grid = (b, s // ts)
    row = lambda bi, i: (bi, i, 0)
    heads = lambda bi, i: (0, bi, 0, i, 0)
    bf_spec = pl.BlockSpec((1, ts, D_ATT), row)
    out_shape = [
        jax.ShapeDtypeStruct((b, s, D_ATT), BF16),
        jax.ShapeDtypeStruct((b, s, D_ATT), BF16),
        jax.ShapeDtypeStruct((b, s, D_ATT), BF16),
        jax.ShapeDtypeStruct((b, s, D_ATT), BF16),
        jax.ShapeDtypeStruct((b, s, D_CONV), F32),
        jax.ShapeDtypeStruct((b, s, D_CONV), BF16),
        jax.ShapeDtypeStruct((1, b, N_HEADS, s, HEAD_DIM), F32),
        jax.ShapeDtypeStruct((1, b, N_HEADS, s, HEAD_DIM), F32),
    ]
    out_specs = [bf_spec, bf_spec, bf_spec, bf_spec,
                 pl.BlockSpec((1, ts, D_CONV), row), bf_spec,
                 pl.BlockSpec((1, 1, N_HEADS, ts, HEAD_DIM), heads),
                 pl.BlockSpec((1, 1, N_HEADS, ts, HEAD_DIM), heads)]
    return pl.pallas_call(
        _in_proj_kernel,
        grid=grid,
        in_specs=[pl.BlockSpec((1, ts, D_MODEL), row),
                  pl.BlockSpec((1, D_MODEL), lambda bi, i: (0, 0)),
                  pl.BlockSpec((D_MODEL, D_IN), lambda bi, i: (0, 0))],
        out_specs=out_specs,
        out_shape=out_shape,
        compiler_params=pltpu.CompilerParams(
            dimension_semantics=("arbitrary", "arbitrary"), vmem_limit_bytes=VMEM_LIMIT),
        name="in_proj",
    )(x, norm_w, w_in_bf)


def _sb_attn_kernel(q_ref, k_ref, v_ref, ga_ref, tri_ref, o_ref, *, tq, tk, q_pos0):
    i = pl.program_id(2)
    q_start = q_pos0 + i * tq
    j_diag = q_start // tk
    lane = lax.broadcasted_iota(jnp.int32, (tq, LANES), 1)
    q2 = q_ref[0]
    zero = jnp.zeros_like(q2)
    qs = (jnp.where(lane < HEAD_DIM, q2, zero), jnp.where(lane >= HEAD_DIM, q2, zero))
    rep = tk // LANES

    def tile(j, state, masked):
        k0 = pl.multiple_of(j * tk, tk)
        kj = k_ref[0, pl.ds(k0, tk), :]
        vj = v_ref[0, pl.ds(k0, tk), :]
        if masked:
            q_pos = q_start + lax.broadcasted_iota(jnp.int32, (tq, tk), 0)
            k_pos = k0 + lax.broadcasted_iota(jnp.int32, (tq, tk), 1)
            valid = k_pos < q_pos
        new = []
        for hh in range(2):
            acc, carry = state[hh]
            z = lax.dot_general(qs[hh], kj, (((1,), (1,)), ((), ())), preferred_element_type=F32)
            lf = jnp.minimum(-z, 0.0) - jnp.log2(1.0 + jnp.exp2(-jnp.abs(z)))
            if masked:
                lf = jnp.where(valid, lf, 0.0)
            cs = jnp.dot(lf.astype(BF16), tri_ref[...], preferred_element_type=F32)
            arg = z + cs[:, :tk] + jnp.concatenate([carry] * rep, axis=1)
            a = jnp.exp2(arg)
            if masked:
                a = jnp.where(valid, a, 0.0)
            acc = acc + jnp.dot(a.astype(BF16), vj, preferred_element_type=F32)
            carry = carry + cs[:, tk:]
            new.append((acc, carry))
        return tuple(new)

    zf = jnp.zeros((tq, LANES), F32)
    state = tile(j_diag, ((zf, zf), (zf, zf)), True)
    state = lax.fori_loop(0, j_diag, lambda t, st: tile(j_diag - 1 - t, st, False), state)
    att = jnp.where(lane < HEAD_DIM, state[0][0], state[1][0])
    o_ref[0] = (att * ga_ref[0].astype(F32)).astype(BF16)


def _tri_ext(tk):
    j = jnp.arange(tk)[:, None]
    s = jnp.arange(tk)[None, :]
    return jnp.concatenate([(j >= s).astype(BF16), jnp.ones((tk, LANES), BF16)], axis=1)


def _sb_attn(q, k, v, ga, tq, tk, q_pos0):
    b, sq, _ = q.shape
    lk = k.shape[1]
    assert tk % tq == 0 and q_pos0 % tq == 0 and lk % tk == 0 and q_pos0 + sq <= lk
    grid = (b, D_ATT // LANES, sq // tq)
    qspec = pl.BlockSpec((1, tq, LANES), lambda bi, p, i: (bi, i, p))
    kspec = pl.BlockSpec((1, lk, LANES), lambda bi, p, i: (bi, 0, p))
    return pl.pallas_call(
        functools.partial(_sb_attn_kernel, tq=tq, tk=tk, q_pos0=q_pos0),
        grid=grid,
        in_specs=[qspec, kspec, kspec, qspec,
                  pl.BlockSpec((tk, tk + LANES), lambda bi, p, i: (0, 0))],
        out_specs=qspec,
        out_shape=jax.ShapeDtypeStruct((b, sq, D_ATT), BF16),
        compiler_params=pltpu.CompilerParams(
            dimension_semantics=("arbitrary", "arbitrary", "arbitrary"), vmem_limit_bytes=VMEM_LIMIT),
        name="sb_attn",
    )(q, k, v, ga, _tri_ext(tk))


def _tail_kernel(x_ref, ma_ref, u_ref, up_ref, hist_ref, gc_ref, dww_ref, dwb_ref, cnw_ref, cnb_ref,
                 wpw_ref, bpw_ref, wout_ref, fnw_ref, y_ref, cs_ref, ext_ref, *, ts):
    i = pl.program_id(1)

    @pl.when(i == 0)
    def _():
        ext_ref[0:HIST] = hist_ref[0]

    @pl.when(i > 0)
    def _():
        ext_ref[0:HIST] = up_ref[0]

    ext_ref[HIST:HIST + ts] = u_ref[0]
    off = HIST - (CONV_W - 1)

    @pl.when(i == pl.num_programs(1) - 1)
    def _():
        cs_ref[0, 0] = ext_ref[ts + off:ts + HIST]

    acc = jnp.zeros((ts, D_CONV), F32) + dwb_ref[...]
    for w in range(CONV_W):
        acc = acc + ext_ref[off + w:off + w + ts] * dww_ref[w:w + 1]
    mu = jnp.mean(acc, axis=-1, keepdims=True)
    xc = acc - mu
    var = jnp.mean(xc * xc, axis=-1, keepdims=True)
    c = xc * lax.rsqrt(var + LN_EPS) * cnw_ref[...] + cnb_ref[...]
    c = _silu(c).astype(BF16)
    conv = jnp.dot(c, wpw_ref[...], preferred_element_type=F32) + bpw_ref[...]
    mc = (conv * gc_ref[0].astype(F32)).astype(BF16)
    mix = (jnp.dot(ma_ref[0], wout_ref[0:D_ATT], preferred_element_type=F32)
           + jnp.dot(mc, wout_ref[D_ATT:], preferred_element_type=F32))
    xn = x_ref[0] + mix
    r = lax.rsqrt(jnp.mean(xn * xn, axis=-1, keepdims=True) + EPS)
    y_ref[0] = xn * r * fnw_ref[...]


def _tail(x, ma, u, hist, gc, dw_w, dw_b, cn_w, cn_b, w_pw_bf, b_pw, w_out_bf, fn_w, ts):
    b, s, _ = x.shape
    assert s % ts == 0 and (ts % HIST == 0 or s == ts)
    grid = (b, s // ts)
    row = lambda bi, i: (bi, i, 0)
    const = lambda bi, i: (0, 0)
    prev = lambda bi, i: (bi, jnp.maximum(i * (ts // HIST) - 1, 0), 0)
    u_prev = u if s > ts else hist
    vec = lambda n: pl.BlockSpec((1, n), const)
    y, cstate = pl.pallas_call(
        functools.partial(_tail_kernel, ts=ts),
        grid=grid,
        in_specs=[pl.BlockSpec((1, ts, D_MODEL), row),
                  pl.BlockSpec((1, ts, D_ATT), row),
                  pl.BlockSpec((1, ts, D_CONV), row),
                  pl.BlockSpec((1, HIST, D_CONV), prev),
                  pl.BlockSpec((1, HIST, D_CONV), lambda bi, i: (bi, 0, 0)),
                  pl.BlockSpec((1, ts, D_CONV), row),
                  pl.BlockSpec((CONV_W, D_CONV), const),
                  vec(D_CONV), vec(D_CONV), vec(D_CONV),
                  pl.BlockSpec((D_CONV, D_CONV), const),
                  vec(D_CONV),
                  pl.BlockSpec((D_MODEL, D_MODEL), const),
                  vec(D_MODEL)],
        out_specs=[pl.BlockSpec((1, ts, D_MODEL), row),
                   pl.BlockSpec((1, 1, CONV_W - 1, D_CONV), lambda bi, i: (0, bi, 0, 0))],
        out_shape=[jax.ShapeDtypeStruct((b, s, D_MODEL), F32),
                   jax.ShapeDtypeStruct((1, b, CONV_W - 1, D_CONV), F32)],
        scratch_shapes=[pltpu.VMEM((HIST + ts, D_CONV), F32)],
        compiler_params=pltpu.CompilerParams(
            dimension_semantics=("arbitrary", "arbitrary"), vmem_limit_bytes=VMEM_LIMIT),
        name="tail",
    )(x, ma, u, u_prev, hist, gc, dw_w, dw_b, cn_w, cn_b, w_pw_bf, b_pw, w_out_bf, fn_w)
    return y, cstate


def _pad_rows(t, n):
    return jnp.pad(t, ((0, 0), (0, n - t.shape[1]), (0, 0)))


def _from_heads_bf(t):
    b, h, s, d = t.shape
    return t.astype(BF16).transpose(0, 2, 1, 3).reshape(b, s, h * d)


def kernel(x_prompt, x_sample, cache_k, cache_v, state_conv, norm_w, w_in, dw_w, dw_b, cn_w, cn_b,
           w_pw, b_pw, w_out, final_norm_w):
    depth = w_in.shape[0]
    assert depth == 1
    l = 0
    b, s, _ = x_prompt.shape
    db, n_new, _ = x_sample.shape
    past = cache_k.shape[3]

    w_in_bf = w_in[l].astype(BF16)
    w_pw_bf = w_pw[l].astype(BF16)
    w_out_bf = w_out[l].astype(BF16)
    nw = norm_w[l][None]
    vecs = (dw_w[l], dw_b[l][None], cn_w[l][None], cn_b[l][None], w_pw_bf, b_pw[l][None], w_out_bf,
            final_norm_w[None])

    ts_p = min(512, s)
    tq_p = min(256, s)
    q, k, v, ga, u, gc, kp, vp = _in_proj(x_prompt, nw, w_in_bf, ts_p)
    ma = _sb_attn(q, k, v, ga, tq_p, tq_p, 0)
    hist0 = jnp.zeros((b, HIST, D_CONV), F32)
    y_p, cp = _tail(x_prompt, ma, u, hist0, gc, *vecs, ts=min(256, s))

    tk_s = 256
    q, k, v, ga, u, gc, ks, vs = _in_proj(x_sample, nw, w_in_bf, n_new)
    lk = -(-(past + n_new) // tk_s) * tk_s
    k_all = _pad_rows(jnp.concatenate([_from_heads_bf(cache_k[l]), k], axis=1), lk)
    v_all = _pad_rows(jnp.concatenate([_from_heads_bf(cache_v[l]), v], axis=1), lk)
    ma = _sb_attn(q, k_all, v_all, ga, n_new, tk_s, past)
    hist_s = jnp.pad(state_conv[l], ((0, 0), (HIST - (CONV_W - 1), 0), (0, 0)))
    y_s, cs = _tail(x_sample, ma, u, hist_s, gc, *vecs, ts=n_new)

    return (y_p, y_s, kp, vp, cp, ks, vs, cs)
```

```python
import functools
import math

import jax
import jax.numpy as jnp
from jax import lax
from jax.experimental import pallas as pl
from jax.experimental.pallas import tpu as pltpu

D_MODEL = 1024
D_ATT = 512
D_CONV = 512
HEAD_DIM = 64
N_HEADS = D_ATT // HEAD_DIM
CONV_W = 31
HIST = 32
EPS = 1e-6
LN_EPS = 1e-5
D_IN = 4 * D_ATT + 3 * D_CONV
LANES = 128
LOG2E = 1.4426950408889634
DEAD_LOG2 = -170.0
VMEM_LIMIT = 56 * 1024 * 1024

BF16 = jnp.bfloat16
F32 = jnp.float32


def _sigmoid(x):
    return 1.0 / (1.0 + jnp.exp(-x))


def _silu(x):
    return x * _sigmoid(x)


def _in_proj_kernel(x_ref, nw_ref, w_ref, q_ref, k_ref, v_ref, ga_ref, u_ref, gc_ref, ko_ref, vo_ref):
    x = x_ref[0]
    r = lax.rsqrt(jnp.mean(x * x, axis=-1, keepdims=True) + EPS)
    h = (x * r * nw_ref[...]).astype(BF16)

    def proj(c):
        return jnp.dot(h, w_ref[:, c * D_ATT:(c + 1) * D_ATT], preferred_element_type=F32)

    q = proj(0)
    q_ref[0] = (q * (-LOG2E * HEAD_DIM ** -0.5)).astype(BF16)
    k = proj(1)
    k_ref[0] = k.astype(BF16)
    for hd in range(N_HEADS):
        ko_ref[0, 0, hd] = k[:, hd * HEAD_DIM:(hd + 1) * HEAD_DIM]
    v = proj(2)
    v_ref[0] = v.astype(BF16)
    for hd in range(N_HEADS):
        vo_ref[0, 0, hd] = v[:, hd * HEAD_DIM:(hd + 1) * HEAD_DIM]
    ga_ref[0] = _silu(proj(3)).astype(BF16)
    u_ref[0] = proj(4) * _sigmoid(proj(5))
    gc_ref[0] = _silu(proj(6)).astype(BF16)


def _in_proj(x, norm_w, w_in_bf, ts):
    b, s, _ = x.shape
    grid = (b, s // ts)
    row = lambda bi, i: (bi, i, 0)
    heads = lambda bi, i: (0, bi, 0, i, 0)
    bf_spec = pl.BlockSpec((1, ts, D_ATT), row)
    out_shape = [
        jax.ShapeDtypeStruct((b, s, D_ATT), BF16),
        jax.ShapeDtypeStruct((b, s, D_ATT), BF16),
        jax.ShapeDtypeStruct((b, s, D_ATT), BF16),
        jax.ShapeDtypeStruct((b, s, D_ATT), BF16),
        jax.ShapeDtypeStruct((b, s, D_CONV), F32),
        jax.ShapeDtypeStruct((b, s, D_CONV), BF16),
        jax.ShapeDtypeStruct((1, b, N_HEADS, s, HEAD_DIM), F32),
        jax.ShapeDtypeStruct((1, b, N_HEADS, s, HEAD_DIM), F32),
    ]
    out_specs = [bf_spec, bf_spec, bf_spec, bf_spec,
                 pl.BlockSpec((1, ts, D_CONV), row), bf_spec,
                 pl.BlockSpec((1, 1, N_HEADS, ts, HEAD_DIM), heads),
                 pl.BlockSpec((1, 1, N_HEADS, ts, HEAD_DIM), heads)]
    return pl.pallas_call(
        _in_proj_kernel,
        grid=grid,
        in_specs=[pl.BlockSpec((1, ts, D_MODEL), row),
                  pl.BlockSpec((1, D_MODEL), lambda bi, i: (0, 0)),
                  pl.BlockSpec((D_MODEL, D_IN), lambda bi, i: (0, 0))],
        out_specs=out_specs,
        out_shape=out_shape,
        compiler_params=pltpu.CompilerParams(
            dimension_semantics=("arbitrary", "arbitrary"), vmem_limit_bytes=VMEM_LIMIT),
        name="in_proj",
    )(x, norm_w, w_in_bf)


def _sb_attn_kernel(q_ref, k_ref, v_ref, ga_ref, tri_ref, o_ref, *, tq, tk, q_pos0):
    i = pl.program_id(2)
    q_start = q_pos0 + i * tq
    j_diag = q_start // tk
    lane = lax.broadcasted_iota(jnp.int32, (tq, LANES), 1)
    q2 = q_ref[0]
    zero = jnp.zeros_like(q2)
    qs = (jnp.where(lane < HEAD_DIM, q2, zero), jnp.where(lane >= HEAD_DIM, q2, zero))
    rep = tk // LANES

    def tile(j, state, masked):
        k0 = pl.multiple_of(j * tk, tk)
        kj = k_ref[0, pl.ds(k0, tk), :]
        vj = v_ref[0, pl.ds(k0, tk), :]
        if masked:
            q_pos = q_start + lax.broadcasted_iota(jnp.int32, (tq, tk), 0)
            k_pos = k0 + lax.broadcasted_iota(jnp.int32, (tq, tk), 1)
            valid = k_pos < q_pos
        new = []
        for hh in range(2):
            acc, carry = state[hh]
            w = lax.dot_general(qs[hh], kj, (((1,), (1,)), ((), ())), preferred_element_type=F32)
            lf = jnp.minimum(w, 0.0) - jnp.log2(1.0 + jnp.exp2(-jnp.abs(w)))
            if masked:
                lf = jnp.where(valid, lf, 0.0)
            cs = jnp.dot(lf.astype(BF16), tri_ref[...], preferred_element_type=F32)
            t1 = cs + jnp.concatenate([carry] * rep, axis=1)
            a = jnp.exp2(t1 - w)
            if masked:
                a = jnp.where(valid, a, 0.0)
            acc = acc + jnp.dot(a.astype(BF16), vj, preferred_element_type=F32)
            carry = jnp.broadcast_to(t1[:, 0:1], (tq, LANES))
            new.append((acc, carry))
        return tuple(new)

    def live(state):
        return (jnp.max(jnp.maximum(state[0][1], state[1][1])) >= DEAD_LOG2).astype(jnp.int32)

    zf = jnp.zeros((tq, LANES), F32)
    state = tile(j_diag, ((zf, zf), (zf, zf)), True)

    def body(c):
        j, _, st = c
        st = tile(j, st, False)
        return j - 1, live(st), st

    _, _, state = lax.while_loop(lambda c: jnp.logical_and(c[0] >= 0, c[1] > 0), body,
                                 (j_diag - 1, live(state), state))
    att = jnp.where(lane < HEAD_DIM, state[0][0], state[1][0])
    o_ref[0] = (att * ga_ref[0].astype(F32)).astype(BF16)


def _tri(tk):
    j = jnp.arange(tk)[:, None]
    s = jnp.arange(tk)[None, :]
    return (j >= s).astype(BF16)


def _sb_attn(q, k, v, ga, tq, tk, q_pos0):
    b, sq, _ = q.shape
    lk = k.shape[1]
    assert tk % tq == 0 and q_pos0 % tq == 0 and lk % tk == 0 and q_pos0 + sq <= lk
    grid = (b, D_ATT // LANES, sq // tq)
    qspec = pl.BlockSpec((1, tq, LANES), lambda bi, p, i: (bi, i, p))
    kspec = pl.BlockSpec((1, lk, LANES), lambda bi, p, i: (bi, 0, p))
    return pl.pallas_call(
        functools.partial(_sb_attn_kernel, tq=tq, tk=tk, q_pos0=q_pos0),
        grid=grid,
        in_specs=[qspec, kspec, kspec, qspec,
                  pl.BlockSpec((tk, tk), lambda bi, p, i: (0, 0))],
        out_specs=qspec,
        out_shape=jax.ShapeDtypeStruct((b, sq, D_ATT), BF16),
        compiler_params=pltpu.CompilerParams(
            dimension_semantics=("arbitrary", "arbitrary", "arbitrary"), vmem_limit_bytes=VMEM_LIMIT),
        name="sb_attn",
    )(q, k, v, ga, _tri(tk))


def _tail_kernel(x_ref, ma_ref, u_ref, up_ref, hist_ref, gc_ref, dww_ref, dwb_ref, cnw_ref, cnb_ref,
                 wpw_ref, bpw_ref, wout_ref, fnw_ref, y_ref, cs_ref, ext_ref, *, ts):
    i = pl.program_id(1)

    @pl.when(i == 0)
    def _():
        ext_ref[0:HIST] = hist_ref[0]

    @pl.when(i > 0)
    def _():
        ext_ref[0:HIST] = up_ref[0]

    ext_ref[HIST:HIST + ts] = u_ref[0]
    off = HIST - (CONV_W - 1)

    @pl.when(i == pl.num_programs(1) - 1)
    def _():
        cs_ref[0, 0] = ext_ref[ts + off:ts + HIST]

    acc = jnp.zeros((ts, D_CONV), F32) + dwb_ref[...]
    for w in range(CONV_W):
        acc = acc + ext_ref[off + w:off + w + ts] * dww_ref[w:w + 1]
    mu = jnp.mean(acc, axis=-1, keepdims=True)
    xc = acc - mu
    var = jnp.mean(xc * xc, axis=-1, keepdims=True)
    c = xc * lax.rsqrt(var + LN_EPS) * cnw_ref[...] + cnb_ref[...]
    c = _silu(c).astype(BF16)
    conv = jnp.dot(c, wpw_ref[...], preferred_element_type=F32) + bpw_ref[...]
    mc = (conv * gc_ref[0].astype(F32)).astype(BF16)
    mix = (jnp.dot(ma_ref[0], wout_ref[0:D_ATT], preferred_element_type=F32)
           + jnp.dot(mc, wout_ref[D_ATT:], preferred_element_type=F32))
    xn = x_ref[0] + mix
    r = lax.rsqrt(jnp.mean(xn * xn, axis=-1, keepdims=True) + EPS)
    y_ref[0] = xn * r * fnw_ref[...]


def _tail(x, ma, u, hist, gc, dw_w, dw_b, cn_w, cn_b, w_pw_bf, b_pw, w_out_bf, fn_w, ts):
    b, s, _ = x.shape
    assert s % ts == 0 and (ts % HIST == 0 or s == ts)
    grid = (b, s // ts)
    row = lambda bi, i: (bi, i, 0)
    const = lambda bi, i: (0, 0)
    prev = lambda bi, i: (bi, jnp.maximum(i * (ts // HIST) - 1, 0), 0)
    u_prev = u if s > ts else hist
    vec = lambda n: pl.BlockSpec((1, n), const)
    y, cstate = pl.pallas_call(
        functools.partial(_tail_kernel, ts=ts),
        grid=grid,
        in_specs=[pl.BlockSpec((1, ts, D_MODEL), row),
                  pl.BlockSpec((1, ts, D_ATT), row),
                  pl.BlockSpec((1, ts, D_CONV), row),
                  pl.BlockSpec((1, HIST, D_CONV), prev),
                  pl.BlockSpec((1, HIST, D_CONV), lambda bi, i: (bi, 0, 0)),
                  pl.BlockSpec((1, ts, D_CONV), row),
                  pl.BlockSpec((CONV_W, D_CONV), const),
                  vec(D_CONV), vec(D_CONV), vec(D_CONV),
                  pl.BlockSpec((D_CONV, D_CONV), const),
                  vec(D_CONV),
                  pl.BlockSpec((D_MODEL, D_MODEL), const),
                  vec(D_MODEL)],
        out_specs=[pl.BlockSpec((1, ts, D_MODEL), row),
                   pl.BlockSpec((1, 1, CONV_W - 1, D_CONV), lambda bi, i: (0, bi, 0, 0))],
        out_shape=[jax.ShapeDtypeStruct((b, s, D_MODEL), F32),
                   jax.ShapeDtypeStruct((1, b, CONV_W - 1, D_CONV), F32)],
        scratch_shapes=[pltpu.VMEM((HIST + ts, D_CONV), F32)],
        compiler_params=pltpu.CompilerParams(
            dimension_semantics=("arbitrary", "arbitrary"), vmem_limit_bytes=VMEM_LIMIT),
        name="tail",
    )(x, ma, u, u_prev, hist, gc, dw_w, dw_b, cn_w, cn_b, w_pw_bf, b_pw, w_out_bf, fn_w)
    return y, cstate


def _pad_rows(t, n):
    return jnp.pad(t, ((0, 0), (0, n - t.shape[1]), (0, 0)))


def _from_heads_bf(t):
    b, h, s, d = t.shape
    return t.astype(BF16).transpose(0, 2, 1, 3).reshape(b, s, h * d)


def kernel(x_prompt, x_sample, cache_k, cache_v, state_conv, norm_w, w_in, dw_w, dw_b, cn_w, cn_b,
           w_pw, b_pw, w_out, final_norm_w):
    depth = w_in.shape[0]
    assert depth == 1
    l = 0
    b, s, _ = x_prompt.shape
    db, n_new, _ = x_sample.shape
    past = cache_k.shape[3]

    w_in_bf = w_in[l].astype(BF16)
    w_pw_bf = w_pw[l].astype(BF16)
    w_out_bf = w_out[l].astype(BF16)
    nw = norm_w[l][None]
    vecs = (dw_w[l], dw_b[l][None], cn_w[l][None], cn_b[l][None], w_pw_bf, b_pw[l][None], w_out_bf,
            final_norm_w[None])

    ts_p = min(512, s)
    tq_p = min(256, s)
    q, k, v, ga, u, gc, kp, vp = _in_proj(x_prompt, nw, w_in_bf, ts_p)
    ma = _sb_attn(q, k, v, ga, tq_p, tq_p, 0)
    hist0 = jnp.zeros((b, HIST, D_CONV), F32)
    y_p, cp = _tail(x_prompt, ma, u, hist0, gc, *vecs, ts=min(256, s))

    tk_s = 256
    q, k, v, ga, u, gc, ks, vs = _in_proj(x_sample, nw, w_in_bf, n_new)
    lk = -(-(past + n_new) // tk_s) * tk_s
    k_all = _pad_rows(jnp.concatenate([_from_heads_bf(cache_k[l]), k], axis=1), lk)
    v_all = _pad_rows(jnp.concatenate([_from_heads_bf(cache_v[l]), v], axis=1), lk)
    ma = _sb_attn(q, k_all, v_all, ga, n_new, tk_s, past)
    hist_s = jnp.pad(state_conv[l], ((0, 0), (HIST - (CONV_W - 1), 0), (0, 0)))
    y_s, cs = _tail(x_sample, ma, u, hist_s, gc, *vecs, ts=n_new)

    return (y_p, y_s, kp, vp, cp, ks, vs, cs)
```

```python
import functools

import jax
import jax.numpy as jnp
from jax import lax
from jax.experimental import pallas as pl
from jax.experimental.pallas import tpu as pltpu

D_MODEL = 1024
D_ATT = 512
D_CONV = 512
HEAD_DIM = 64
N_HEADS = D_ATT // HEAD_DIM
CONV_W = 31
SUBLANES = 8
LANES = 128
HIST = 32
HIST_OFF = HIST - (CONV_W - 1)
CONV_ROWS = 64
EPS = 1e-6
LN_EPS = 1e-5
D_IN = 4 * D_ATT + 3 * D_CONV
LOG2E = 1.4426950408889634
DEAD_LOG2 = -170.0
VMEM_LIMIT = 56 * 1024 * 1024

BF16 = jnp.bfloat16
F32 = jnp.float32


def _sigmoid(x):
    return 1.0 / (1.0 + jnp.exp(-x))


def _silu(x):
    return x * _sigmoid(x)


def _in_proj_kernel(x_ref, nw_ref, w_ref, hist_ref, dww_ref, dwb_ref, cnw_ref, cnb_ref,
                    q_ref, k_ref, v_ref, ga_ref, cn_ref, gc_ref, ko_ref, vo_ref, cs_ref,
                    ext_ref, sh_ref, *, ts, multi_block):
    i = pl.program_id(1)
    x = x_ref[0]
    r = lax.rsqrt(jnp.mean(x * x, axis=-1, keepdims=True) + EPS)
    h = (x * r * nw_ref[...]).astype(BF16)

    def proj(c):
        return jnp.dot(h, w_ref[:, c * D_ATT:(c + 1) * D_ATT], preferred_element_type=F32)

    @pl.when(i == 0)
    def _():
        ext_ref[0:HIST] = hist_ref[0]

    ext_ref[HIST:HIST + ts] = proj(4) * _sigmoid(proj(5))

    @pl.when(i == pl.num_programs(1) - 1)
    def _():
        cs_ref[0, 0] = ext_ref[ts + HIST_OFF:ts + HIST]

    n_sh = ts + HIST - SUBLANES
    for s in range(1, SUBLANES):
        sh_ref[s - 1] = ext_ref[s:s + n_sh]

    def conv_chunk(c0, rc):
        acc = jnp.broadcast_to(dwb_ref[...][None], (rc // SUBLANES, SUBLANES, D_CONV))
        for w in range(CONV_W):
            s = (HIST_OFF + w) % SUBLANES
            a0 = c0 + HIST_OFF + w - s
            src = ext_ref[a0:a0 + rc] if s == 0 else sh_ref[s - 1, a0:a0 + rc]
            acc = acc + src.reshape(rc // SUBLANES, SUBLANES, D_CONV) * dww_ref[w][None]
        acc = acc.reshape(rc, D_CONV)
        mu = jnp.mean(acc, axis=-1, keepdims=True)
        xc = acc - mu
        var = jnp.mean(xc * xc, axis=-1, keepdims=True)
        cn = _silu(xc * lax.rsqrt(var + LN_EPS) * cnw_ref[...] + cnb_ref[...])
        cn_ref[0, c0:c0 + rc] = cn.astype(BF16)

    rc = min(CONV_ROWS, ts)
    for c0 in range(0, ts, rc):
        conv_chunk(c0, rc)

    q_ref[0] = (proj(0) * (-LOG2E * HEAD_DIM ** -0.5)).astype(BF16)
    for c, bf_ref, heads_ref in ((1, k_ref, ko_ref), (2, v_ref, vo_ref)):
        t = proj(c)
        bf_ref[0] = t.astype(BF16)
        for hd in range(N_HEADS):
            heads_ref[0, 0, hd] = t[:, hd * HEAD_DIM:(hd + 1) * HEAD_DIM]
    ga_ref[0] = _silu(proj(3)).astype(BF16)
    gc_ref[0] = _silu(proj(6)).astype(BF16)

    if multi_block:
        ext_ref[0:HIST] = ext_ref[ts:ts + HIST]


def _in_proj(x, norm_w, w_in_bf, hist, dw_w, dw_b, cn_w, cn_b, ts):
    b, s, _ = x.shape
    assert s % ts == 0 and (ts >= HIST or s == ts)
    grid = (b, s // ts)
    row = lambda bi, i: (bi, i, 0)
    const = lambda bi, i: (0, 0)
    heads = lambda bi, i: (0, bi, 0, i, 0)
    bf_spec = pl.BlockSpec((1, ts, D_ATT), row)
    vec = pl.BlockSpec((1, D_CONV), const)
    out_shape = [
        jax.ShapeDtypeStruct((b, s, D_ATT), BF16),
        jax.ShapeDtypeStruct((b, s, D_ATT), BF16),
        jax.ShapeDtypeStruct((b, s, D_ATT), BF16),
        jax.ShapeDtypeStruct((b, s, D_ATT), BF16),
        jax.ShapeDtypeStruct((b, s, D_CONV), BF16),
        jax.ShapeDtypeStruct((b, s, D_CONV), BF16),
        jax.ShapeDtypeStruct((1, b, N_HEADS, s, HEAD_DIM), F32),
        jax.ShapeDtypeStruct((1, b, N_HEADS, s, HEAD_DIM), F32),
        jax.ShapeDtypeStruct((1, b, CONV_W - 1, D_CONV), F32),
    ]
    out_specs = [bf_spec] * 6 + [
        pl.BlockSpec((1, 1, N_HEADS, ts, HEAD_DIM), heads),
        pl.BlockSpec((1, 1, N_HEADS, ts, HEAD_DIM), heads),
        pl.BlockSpec((1, 1, CONV_W - 1, D_CONV), lambda bi, i: (0, bi, 0, 0))]
    return pl.pallas_call(
        functools.partial(_in_proj_kernel, ts=ts, multi_block=s > ts),
        grid=grid,
        in_specs=[pl.BlockSpec((1, ts, D_MODEL), row),
                  pl.BlockSpec((1, D_MODEL), const),
                  pl.BlockSpec((D_MODEL, D_IN), const),
                  pl.BlockSpec((1, HIST, D_CONV), lambda bi, i: (bi, 0, 0)),
                  pl.BlockSpec((CONV_W, SUBLANES, D_CONV), lambda bi, i: (0, 0, 0)),
                  pl.BlockSpec((SUBLANES, D_CONV), const),
                  vec, vec],
        out_specs=out_specs,
        out_shape=out_shape,
        scratch_shapes=[pltpu.VMEM((HIST + ts, D_CONV), F32),
                        pltpu.VMEM((SUBLANES - 1, HIST + ts - SUBLANES, D_CONV), F32)],
        compiler_params=pltpu.CompilerParams(
            dimension_semantics=("arbitrary", "arbitrary"), vmem_limit_bytes=VMEM_LIMIT),
        name="in_proj",
    )(x, norm_w, w_in_bf, hist, dw_w, dw_b, cn_w, cn_b)


def _sb_attn_kernel(q_ref, k_ref, v_ref, ga_ref, tri_ref, o_ref, qm_ref, acc_ref, carry_ref,
                    *, tq, tk, q_pos0, n_pair):
    i = pl.program_id(2)
    n_head = 2 * n_pair
    q_start = q_pos0 + i * tq
    j_diag = q_start // tk
    lane = lax.broadcasted_iota(jnp.int32, (tq, LANES), 1)
    zero = jnp.zeros((tq, LANES), BF16)
    for p in range(n_pair):
        q2 = q_ref[0, :, p * LANES:(p + 1) * LANES]
        qm_ref[2 * p] = jnp.where(lane < HEAD_DIM, q2, zero)
        qm_ref[2 * p + 1] = jnp.where(lane >= HEAD_DIM, q2, zero)
    acc_ref[...] = jnp.zeros_like(acc_ref)
    carry_ref[...] = jnp.zeros_like(carry_ref)
    rep = tk // LANES
    heads = range(n_head)
    cols = [slice((hh // 2) * LANES, (hh // 2 + 1) * LANES) for hh in heads]

    def tile(j, masked):
        k0 = pl.multiple_of(j * tk, tk)
        kj = k_ref[0, pl.ds(k0, tk), :]
        vj = v_ref[0, pl.ds(k0, tk), :]
        if masked:
            q_pos = q_start + lax.broadcasted_iota(jnp.int32, (tq, tk), 0)
            k_pos = k0 + lax.broadcasted_iota(jnp.int32, (tq, tk), 1)
            valid = k_pos < q_pos
        ws = [lax.dot_general(qm_ref[hh], kj[:, cols[hh]], (((1,), (1,)), ((), ())),
                              preferred_element_type=F32) for hh in heads]
        lfs = []
        for hh in heads:
            w = ws[hh]
            lf = jnp.minimum(w, 0.0) - jnp.log2(1.0 + jnp.exp2(jnp.minimum(w, -w)))
            if masked:
                lf = jnp.where(valid, lf, 0.0)
            lfs.append(lf.astype(BF16))
        css = [jnp.dot(lfs[hh], tri_ref[...], preferred_element_type=F32) for hh in heads]
        ats = []
        for hh in heads:
            t1 = css[hh] + jnp.concatenate([carry_ref[hh]] * rep, axis=1)
            a = jnp.exp2(t1 - ws[hh])
            if masked:
                a = jnp.where(valid, a, 0.0)
            ats.append(a.astype(BF16))
            carry_ref[hh] = jnp.broadcast_to(t1[:, 0:1], (tq, LANES))
        for hh in heads:
            acc_ref[hh] += jnp.dot(ats[hh], vj[:, cols[hh]], preferred_element_type=F32)

    def live():
        m = carry_ref[0]
        for hh in range(1, n_head):
            m = jnp.maximum(m, carry_ref[hh])
        return (jnp.max(m) >= DEAD_LOG2).astype(jnp.int32)

    tile(j_diag, True)

    def body(c):
        j, _ = c
        tile(j, False)
        return j - 1, live()

    lax.while_loop(lambda c: jnp.logical_and(c[0] >= 0, c[1] > 0), body, (j_diag - 1, live()))
    for p in range(n_pair):
        att = jnp.where(lane < HEAD_DIM, acc_ref[2 * p], acc_ref[2 * p + 1])
        pc = slice(p * LANES, (p + 1) * LANES)
        o_ref[0, :, pc] = (att * ga_ref[0, :, pc].astype(F32)).astype(BF16)


def _tri(tk):
    j = jnp.arange(tk)[:, None]
    s = jnp.arange(tk)[None, :]
    return (j >= s).astype(BF16)


def _sb_attn(q, k, v, ga, tq, tk, q_pos0, n_pair):
    b, sq, _ = q.shape
    lk = k.shape[1]
    assert tk % tq == 0 and q_pos0 % tq == 0 and lk % tk == 0 and q_pos0 + sq <= lk
    width = n_pair * LANES
    grid = (b, D_ATT // width, sq // tq)
    qspec = pl.BlockSpec((1, tq, width), lambda bi, p, i: (bi, i, p))
    kspec = pl.BlockSpec((1, lk, width), lambda bi, p, i: (bi, 0, p))
    return pl.pallas_call(
        functools.partial(_sb_attn_kernel, tq=tq, tk=tk, q_pos0=q_pos0, n_pair=n_pair),
        grid=grid,
        in_specs=[qspec, kspec, kspec, qspec,
                  pl.BlockSpec((tk, tk), lambda bi, p, i: (0, 0))],
        out_specs=qspec,
        out_shape=jax.ShapeDtypeStruct((b, sq, D_ATT), BF16),
        scratch_shapes=[pltpu.VMEM((2 * n_pair, tq, LANES), BF16),
                        pltpu.VMEM((2 * n_pair, tq, LANES), F32),
                        pltpu.VMEM((2 * n_pair, tq, LANES), F32)],
        compiler_params=pltpu.CompilerParams(
            dimension_semantics=("arbitrary", "arbitrary", "arbitrary"), vmem_limit_bytes=VMEM_LIMIT),
        name="sb_attn",
    )(q, k, v, ga, _tri(tk))


def _tail_kernel(x_ref, ma_ref, cn_ref, gc_ref, wpw_ref, bpw_ref, wout_ref, fnw_ref, y_ref):
    conv = jnp.dot(cn_ref[0], wpw_ref[...], preferred_element_type=F32) + bpw_ref[...]
    mc = (conv * gc_ref[0].astype(F32)).astype(BF16)
    mix = (jnp.dot(ma_ref[0], wout_ref[0:D_ATT], preferred_element_type=F32)
           + jnp.dot(mc, wout_ref[D_ATT:], preferred_element_type=F32))
    xn = x_ref[0] + mix
    r = lax.rsqrt(jnp.mean(xn * xn, axis=-1, keepdims=True) + EPS)
    y_ref[0] = xn * r * fnw_ref[...]


def _tail(x, ma, cn, gc, w_pw_bf, b_pw, w_out_bf, fn_w, ts):
    b, s, _ = x.shape
    assert s % ts == 0
    row = lambda bi, i: (bi, i, 0)
    const = lambda bi, i: (0, 0)
    half = pl.BlockSpec((1, ts, D_ATT), row)
    return pl.pallas_call(
        _tail_kernel,
        grid=(b, s // ts),
        in_specs=[pl.BlockSpec((1, ts, D_MODEL), row), half, half, half,
                  pl.BlockSpec((D_CONV, D_CONV), const),
                  pl.BlockSpec((1, D_CONV), const),
                  pl.BlockSpec((D_MODEL, D_MODEL), const),
                  pl.BlockSpec((1, D_MODEL), const)],
        out_specs=pl.BlockSpec((1, ts, D_MODEL), row),
        out_shape=jax.ShapeDtypeStruct((b, s, D_MODEL), F32),
        compiler_params=pltpu.CompilerParams(
            dimension_semantics=("arbitrary", "arbitrary"), vmem_limit_bytes=VMEM_LIMIT),
        name="tail",
    )(x, ma, cn, gc, w_pw_bf, b_pw, w_out_bf, fn_w)


def _pad_rows(t, n):
    return jnp.pad(t, ((0, 0), (0, n - t.shape[1]), (0, 0)))


def _from_heads_bf(t):
    b, h, s, d = t.shape
    return t.astype(BF16).transpose(0, 2, 1, 3).reshape(b, s, h * d)


def kernel(x_prompt, x_sample, cache_k, cache_v, state_conv, norm_w, w_in, dw_w, dw_b, cn_w, cn_b,
           w_pw, b_pw, w_out, final_norm_w):
    depth = w_in.shape[0]
    assert depth == 1
    l = 0
    b, s, _ = x_prompt.shape
    db, n_new, _ = x_sample.shape
    past = cache_k.shape[3]

    w_in_bf = w_in[l].astype(BF16)
    conv_w = (jnp.broadcast_to(dw_w[l][:, None, :], (CONV_W, SUBLANES, D_CONV)),
              jnp.broadcast_to(dw_b[l][None], (SUBLANES, D_CONV)), cn_w[l][None], cn_b[l][None])
    tail_w = (w_pw[l].astype(BF16), b_pw[l][None], w_out[l].astype(BF16), final_norm_w[None])
    nw = norm_w[l][None]

    tq_p = min(256, s)
    hist0 = jnp.zeros((b, HIST, D_CONV), F32)
    q, k, v, ga, cn, gc, kp, vp, cp = _in_proj(x_prompt, nw, w_in_bf, hist0, *conv_w, ts=min(512, s))
    ma = _sb_attn(q, k, v, ga, tq_p, tq_p, 0, n_pair=2)
    y_p = _tail(x_prompt, ma, cn, gc, *tail_w, ts=min(512, s))

    tk_s = 256
    hist_s = jnp.pad(state_conv[l], ((0, 0), (HIST_OFF, 0), (0, 0)))
    q, k, v, ga, cn, gc, ks, vs, cs = _in_proj(x_sample, nw, w_in_bf, hist_s, *conv_w, ts=n_new)
    lk = -(-(past + n_new) // tk_s) * tk_s
    k_all = _pad_rows(jnp.concatenate([_from_heads_bf(cache_k[l]), k], axis=1), lk)
    v_all = _pad_rows(jnp.concatenate([_from_heads_bf(cache_v[l]), v], axis=1), lk)
    ma = _sb_attn(q, k_all, v_all, ga, n_new, tk_s, past, n_pair=2)
    y_s = _tail(x_sample, ma, cn, gc, *tail_w, ts=n_new)

    return (y_p, y_s, kp, vp, cp, ks, vs, cs)
```

```python
import functools

import jax
import jax.numpy as jnp
from jax import lax
from jax.experimental import pallas as pl
from jax.experimental.pallas import tpu as pltpu

D_MODEL = 1024
D_ATT = 512
D_CONV = 512
HEAD_DIM = 64
N_HEADS = D_ATT // HEAD_DIM
CONV_W = 31
SUBLANES = 8
LANES = 128
HIST = 32
HIST_OFF = HIST - (CONV_W - 1)
CONV_ROWS = 64
EPS = 1e-6
LN_EPS = 1e-5
D_IN = 4 * D_ATT + 3 * D_CONV
LOG2E = 1.4426950408889634
DEAD_LOG2 = -170.0
VMEM_LIMIT = 56 * 1024 * 1024

BF16 = jnp.bfloat16
F32 = jnp.float32


def _sigmoid(x):
    return 1.0 / (1.0 + jnp.exp(-x))


def _silu(x):
    return x * _sigmoid(x)


def _in_proj_kernel(x_ref, nw_ref, w_ref, hist_ref, dww_ref, dwb_ref, cnw_ref, cnb_ref,
                    q_ref, k_ref, v_ref, ga_ref, cn_ref, gc_ref, ko_ref, vo_ref, cs_ref,
                    ext_ref, sh_ref, *, ts, multi_block):
    i = pl.program_id(1)
    x = x_ref[0]
    r = lax.rsqrt(jnp.mean(x * x, axis=-1, keepdims=True) + EPS)
    h = (x * r * nw_ref[...]).astype(BF16)

    def proj(c):
        return jnp.dot(h, w_ref[:, c * D_ATT:(c + 1) * D_ATT], preferred_element_type=F32)

    @pl.when(i == 0)
    def _():
        ext_ref[0:HIST] = hist_ref[0]

    ext_ref[HIST:HIST + ts] = proj(4) * _sigmoid(proj(5))

    @pl.when(i == pl.num_programs(1) - 1)
    def _():
        cs_ref[0, 0] = ext_ref[ts + HIST_OFF:ts + HIST]

    n_sh = ts + HIST - SUBLANES
    for s in range(1, SUBLANES):
        sh_ref[s - 1] = ext_ref[s:s + n_sh]

    def conv_chunk(c0, rc):
        acc = jnp.broadcast_to(dwb_ref[...][None], (rc // SUBLANES, SUBLANES, D_CONV))
        for w in range(CONV_W):
            s = (HIST_OFF + w) % SUBLANES
            a0 = c0 + HIST_OFF + w - s
            src = ext_ref[a0:a0 + rc] if s == 0 else sh_ref[s - 1, a0:a0 + rc]
            acc = acc + src.reshape(rc // SUBLANES, SUBLANES, D_CONV) * dww_ref[w][None]
        acc = acc.reshape(rc, D_CONV)
        mu = jnp.mean(acc, axis=-1, keepdims=True)
        xc = acc - mu
        var = jnp.mean(xc * xc, axis=-1, keepdims=True)
        cn = _silu(xc * lax.rsqrt(var + LN_EPS) * cnw_ref[...] + cnb_ref[...])
        cn_ref[0, c0:c0 + rc] = cn.astype(BF16)

    rc = min(CONV_ROWS, ts)
    for c0 in range(0, ts, rc):
        conv_chunk(c0, rc)

    q_ref[0] = (proj(0) * (-LOG2E * HEAD_DIM ** -0.5)).astype(BF16)
    for c, bf_ref, heads_ref in ((1, k_ref, ko_ref), (2, v_ref, vo_ref)):
        t = proj(c)
        bf_ref[0] = t.astype(BF16)
        for hd in range(N_HEADS):
            heads_ref[0, 0, hd] = t[:, hd * HEAD_DIM:(hd + 1) * HEAD_DIM]
    ga_ref[0] = _silu(proj(3)).astype(BF16)
    gc_ref[0] = _silu(proj(6)).astype(BF16)

    if multi_block:
        ext_ref[0:HIST] = ext_ref[ts:ts + HIST]


def _in_proj(x, norm_w, w_in_bf, hist, dw_w, dw_b, cn_w, cn_b, ts):
    b, s, _ = x.shape
    assert s % ts == 0 and (ts >= HIST or s == ts)
    grid = (b, s // ts)
    row = lambda bi, i: (bi, i, 0)
    const = lambda bi, i: (0, 0)
    heads = lambda bi, i: (0, bi, 0, i, 0)
    bf_spec = pl.BlockSpec((1, ts, D_ATT), row)
    vec = pl.BlockSpec((1, D_CONV), const)
    out_shape = [
        jax.ShapeDtypeStruct((b, s, D_ATT), BF16),
        jax.ShapeDtypeStruct((b, s, D_ATT), BF16),
        jax.ShapeDtypeStruct((b, s, D_ATT), BF16),
        jax.ShapeDtypeStruct((b, s, D_ATT), BF16),
        jax.ShapeDtypeStruct((b, s, D_CONV), BF16),
        jax.ShapeDtypeStruct((b, s, D_CONV), BF16),
        jax.ShapeDtypeStruct((1, b, N_HEADS, s, HEAD_DIM), F32),
        jax.ShapeDtypeStruct((1, b, N_HEADS, s, HEAD_DIM), F32),
        jax.ShapeDtypeStruct((1, b, CONV_W - 1, D_CONV), F32),
    ]
    out_specs = [bf_spec] * 6 + [
        pl.BlockSpec((1, 1, N_HEADS, ts, HEAD_DIM), heads),
        pl.BlockSpec((1, 1, N_HEADS, ts, HEAD_DIM), heads),
        pl.BlockSpec((1, 1, CONV_W - 1, D_CONV), lambda bi, i: (0, bi, 0, 0))]
    return pl.pallas_call(
        functools.partial(_in_proj_kernel, ts=ts, multi_block=s > ts),
        grid=grid,
        in_specs=[pl.BlockSpec((1, ts, D_MODEL), row),
                  pl.BlockSpec((1, D_MODEL), const),
                  pl.BlockSpec((D_MODEL, D_IN), const),
                  pl.BlockSpec((1, HIST, D_CONV), lambda bi, i: (bi, 0, 0)),
                  pl.BlockSpec((CONV_W, SUBLANES, D_CONV), lambda bi, i: (0, 0, 0)),
                  pl.BlockSpec((SUBLANES, D_CONV), const),
                  vec, vec],
        out_specs=out_specs,
        out_shape=out_shape,
        scratch_shapes=[pltpu.VMEM((HIST + ts, D_CONV), F32),
                        pltpu.VMEM((SUBLANES - 1, HIST + ts - SUBLANES, D_CONV), F32)],
        compiler_params=pltpu.CompilerParams(
            dimension_semantics=("arbitrary", "arbitrary"), vmem_limit_bytes=VMEM_LIMIT),
        name="in_proj",
    )(x, norm_w, w_in_bf, hist, dw_w, dw_b, cn_w, cn_b)


def _sb_attn_kernel(q_ref, k_ref, v_ref, ga_ref, tri_ref, o_ref, qm_ref, acc_ref, carry_ref, live_ref,
                    *, tq, tk, q_pos0, n_pair, top):
    i = pl.program_id(2)
    n_head = 2 * n_pair
    q_start = q_pos0 + i * tq
    j_diag = q_start // tk
    lane = lax.broadcasted_iota(jnp.int32, (tq, LANES), 1)
    zero = jnp.zeros((tq, LANES), BF16)
    for p in range(n_pair):
        q2 = q_ref[0, :, p * LANES:(p + 1) * LANES]
        qm_ref[2 * p] = jnp.where(lane < HEAD_DIM, q2, zero)
        qm_ref[2 * p + 1] = jnp.where(lane >= HEAD_DIM, q2, zero)
    acc_ref[...] = jnp.zeros_like(acc_ref)
    carry_ref[...] = jnp.zeros_like(carry_ref)
    rep = tk // LANES
    heads = range(n_head)
    cols = [slice((hh // 2) * LANES, (hh // 2 + 1) * LANES) for hh in heads]

    def tile(j, masked, rows):
        k0 = pl.multiple_of(j * tk, tk)
        kj = k_ref[0, pl.ds(k0, tk), :]
        vj = v_ref[0, pl.ds(k0, tk), :]
        if masked:
            q_pos = q_start + lax.broadcasted_iota(jnp.int32, (rows, tk), 0)
            k_pos = k0 + lax.broadcasted_iota(jnp.int32, (rows, tk), 1)
            valid = k_pos < q_pos
        ws = [lax.dot_general(qm_ref[hh, 0:rows], kj[:, cols[hh]], (((1,), (1,)), ((), ())),
                              preferred_element_type=F32) for hh in heads]
        lfs = []
        for hh in heads:
            w = ws[hh]
            lf = jnp.minimum(w, 0.0) - jnp.log2(1.0 + jnp.exp2(jnp.minimum(w, -w)))
            if masked:
                lf = jnp.where(valid, lf, 0.0)
            lfs.append(lf.astype(BF16))
        css = [jnp.dot(lfs[hh], tri_ref[...], preferred_element_type=F32) for hh in heads]
        ats = []
        worst = None
        for hh in heads:
            t1 = css[hh] + jnp.concatenate([carry_ref[hh, 0:rows]] * rep, axis=1)
            a = jnp.exp2(t1 - ws[hh])
            if masked:
                a = jnp.where(valid, a, 0.0)
            ats.append(a.astype(BF16))
            total = t1[:, 0:1]
            carry_ref[hh, 0:rows] = jnp.broadcast_to(total, (rows, LANES))
            worst = total if worst is None else jnp.maximum(worst, total)
        for hh in heads:
            acc_ref[hh, 0:rows] += jnp.dot(ats[hh], vj[:, cols[hh]], preferred_element_type=F32)
        live_ref[0] = (jnp.max(worst[0:top]) >= DEAD_LOG2).astype(jnp.int32)
        live_ref[1] = (jnp.max(worst[top:rows]) >= DEAD_LOG2).astype(jnp.int32) if rows > top else jnp.int32(0)

    tile(j_diag, True, tq)

    def body(c):
        j, _, live_bottom = c
        if top < tq:
            @pl.when(live_bottom > 0)
            def _():
                tile(j, False, tq)

            @pl.when(live_bottom == 0)
            def _():
                tile(j, False, top)
        else:
            tile(j, False, tq)
        return j - 1, live_ref[0], live_ref[1]

    lax.while_loop(lambda c: jnp.logical_and(c[0] >= 0, c[1] + c[2] > 0), body,
                   (j_diag - 1, live_ref[0], live_ref[1]))
    for p in range(n_pair):
        att = jnp.where(lane < HEAD_DIM, acc_ref[2 * p], acc_ref[2 * p + 1])
        pc = slice(p * LANES, (p + 1) * LANES)
        o_ref[0, :, pc] = (att * ga_ref[0, :, pc].astype(F32)).astype(BF16)


def _tri(tk):
    j = jnp.arange(tk)[:, None]
    s = jnp.arange(tk)[None, :]
    return (j >= s).astype(BF16)


def _sb_attn(q, k, v, ga, tq, tk, q_pos0, n_pair):
    b, sq, _ = q.shape
    lk = k.shape[1]
    assert tk % tq == 0 and q_pos0 % tq == 0 and lk % tk == 0 and q_pos0 + sq <= lk
    width = n_pair * LANES
    top = 3 * tq // 4 if tq % 64 == 0 else tq
    grid = (b, D_ATT // width, sq // tq)
    qspec = pl.BlockSpec((1, tq, width), lambda bi, p, i: (bi, i, p))
    kspec = pl.BlockSpec((1, lk, width), lambda bi, p, i: (bi, 0, p))
    return pl.pallas_call(
        functools.partial(_sb_attn_kernel, tq=tq, tk=tk, q_pos0=q_pos0, n_pair=n_pair, top=top),
        grid=grid,
        in_specs=[qspec, kspec, kspec, qspec,
                  pl.BlockSpec((tk, tk), lambda bi, p, i: (0, 0))],
        out_specs=qspec,
        out_shape=jax.ShapeDtypeStruct((b, sq, D_ATT), BF16),
        scratch_shapes=[pltpu.VMEM((2 * n_pair, tq, LANES), BF16),
                        pltpu.VMEM((2 * n_pair, tq, LANES), F32),
                        pltpu.VMEM((2 * n_pair, tq, LANES), F32),
                        pltpu.SMEM((2,), jnp.int32)],
        compiler_params=pltpu.CompilerParams(
            dimension_semantics=("arbitrary", "arbitrary", "arbitrary"), vmem_limit_bytes=VMEM_LIMIT),
        name="sb_attn",
    )(q, k, v, ga, _tri(tk))


def _tail_kernel(x_ref, ma_ref, cn_ref, gc_ref, wpw_ref, bpw_ref, wout_ref, fnw_ref, y_ref):
    conv = jnp.dot(cn_ref[0], wpw_ref[...], preferred_element_type=F32) + bpw_ref[...]
    mc = (conv * gc_ref[0].astype(F32)).astype(BF16)
    mix = (jnp.dot(ma_ref[0], wout_ref[0:D_ATT], preferred_element_type=F32)
           + jnp.dot(mc, wout_ref[D_ATT:], preferred_element_type=F32))
    xn = x_ref[0] + mix
    r = lax.rsqrt(jnp.mean(xn * xn, axis=-1, keepdims=True) + EPS)
    y_ref[0] = xn * r * fnw_ref[...]


def _tail(x, ma, cn, gc, w_pw_bf, b_pw, w_out_bf, fn_w, ts):
    b, s, _ = x.shape
    assert s % ts == 0
    row = lambda bi, i: (bi, i, 0)
    const = lambda bi, i: (0, 0)
    half = pl.BlockSpec((1, ts, D_ATT), row)
    return pl.pallas_call(
        _tail_kernel,
        grid=(b, s // ts),
        in_specs=[pl.BlockSpec((1, ts, D_MODEL), row), half, half, half,
                  pl.BlockSpec((D_CONV, D_CONV), const),
                  pl.BlockSpec((1, D_CONV), const),
                  pl.BlockSpec((D_MODEL, D_MODEL), const),
                  pl.BlockSpec((1, D_MODEL), const)],
        out_specs=pl.BlockSpec((1, ts, D_MODEL), row),
        out_shape=jax.ShapeDtypeStruct((b, s, D_MODEL), F32),
        compiler_params=pltpu.CompilerParams(
            dimension_semantics=("arbitrary", "arbitrary"), vmem_limit_bytes=VMEM_LIMIT),
        name="tail",
    )(x, ma, cn, gc, w_pw_bf, b_pw, w_out_bf, fn_w)


def _pad_rows(t, n):
    return jnp.pad(t, ((0, 0), (0, n - t.shape[1]), (0, 0)))


def _from_heads_bf(t):
    b, h, s, d = t.shape
    return t.astype(BF16).transpose(0, 2, 1, 3).reshape(b, s, h * d)


def kernel(x_prompt, x_sample, cache_k, cache_v, state_conv, norm_w, w_in, dw_w, dw_b, cn_w, cn_b,
           w_pw, b_pw, w_out, final_norm_w):
    depth = w_in.shape[0]
    assert depth == 1
    l = 0
    b, s, _ = x_prompt.shape
    db, n_new, _ = x_sample.shape
    past = cache_k.shape[3]

    w_in_bf = w_in[l].astype(BF16)
    conv_w = (jnp.broadcast_to(dw_w[l][:, None, :], (CONV_W, SUBLANES, D_CONV)),
              jnp.broadcast_to(dw_b[l][None], (SUBLANES, D_CONV)), cn_w[l][None], cn_b[l][None])
    tail_w = (w_pw[l].astype(BF16), b_pw[l][None], w_out[l].astype(BF16), final_norm_w[None])
    nw = norm_w[l][None]

    tq_p = min(256, s)
    hist0 = jnp.zeros((b, HIST, D_CONV), F32)
    q, k, v, ga, cn, gc, kp, vp, cp = _in_proj(x_prompt, nw, w_in_bf, hist0, *conv_w, ts=min(512, s))
    ma = _sb_attn(q, k, v, ga, tq_p, tq_p, 0, n_pair=2)
    y_p = _tail(x_prompt, ma, cn, gc, *tail_w, ts=min(512, s))

    tk_s = 256
    hist_s = jnp.pad(state_conv[l], ((0, 0), (HIST_OFF, 0), (0, 0)))
    q, k, v, ga, cn, gc, ks, vs, cs = _in_proj(x_sample, nw, w_in_bf, hist_s, *conv_w, ts=n_new)
    lk = -(-(past + n_new) // tk_s) * tk_s
    k_all = _pad_rows(jnp.concatenate([_from_heads_bf(cache_k[l]), k], axis=1), lk)
    v_all = _pad_rows(jnp.concatenate([_from_heads_bf(cache_v[l]), v], axis=1), lk)
    ma = _sb_attn(q, k_all, v_all, ga, n_new, tk_s, past, n_pair=2)
    y_s = _tail(x_sample, ma, cn, gc, *tail_w, ts=n_new)

    return (y_p, y_s, kp, vp, cp, ks, vs, cs)
```

```python
import functools

import jax
import jax.numpy as jnp
from jax import lax
from jax.experimental import pallas as pl
from jax.experimental.pallas import tpu as pltpu

D_MODEL = 1024
D_ATT = 512
D_CONV = 512
HEAD_DIM = 64
N_HEADS = D_ATT // HEAD_DIM
CONV_W = 31
SUBLANES = 8
LANES = 128
HIST = 32
HIST_OFF = HIST - (CONV_W - 1)
CONV_ROWS = 64
EPS = 1e-6
LN_EPS = 1e-5
D_IN = 4 * D_ATT + 3 * D_CONV
LOG2E = 1.4426950408889634
DEAD_LOG2 = -170.0
VMEM_LIMIT = 56 * 1024 * 1024

BF16 = jnp.bfloat16
F32 = jnp.float32


def _sigmoid(x):
    return 1.0 / (1.0 + jnp.exp(-x))


def _silu(x):
    return x * _sigmoid(x)


def _in_proj_kernel(x_ref, nw_ref, w_ref, hist_ref, dww_ref, dwb_ref, cnw_ref, cnb_ref,
                    q_ref, k_ref, v_ref, ga_ref, cn_ref, gc_ref, ko_ref, vo_ref, cs_ref,
                    ext_ref, sh_ref, *, ts, multi_block):
    i = pl.program_id(1)
    x = x_ref[0]
    r = lax.rsqrt(jnp.mean(x * x, axis=-1, keepdims=True) + EPS)
    h = (x * r * nw_ref[...]).astype(BF16)

    def proj(c):
        return jnp.dot(h, w_ref[:, c * D_ATT:(c + 1) * D_ATT], preferred_element_type=F32)

    @pl.when(i == 0)
    def _():
        ext_ref[0:HIST] = hist_ref[0]

    ext_ref[HIST:HIST + ts] = proj(4) * _sigmoid(proj(5))

    @pl.when(i == pl.num_programs(1) - 1)
    def _():
        cs_ref[0, 0] = ext_ref[ts + HIST_OFF:ts + HIST]

    n_sh = ts + HIST - SUBLANES
    for s in range(1, SUBLANES):
        sh_ref[s - 1] = ext_ref[s:s + n_sh]

    def conv_chunk(c0, rc):
        acc = jnp.broadcast_to(dwb_ref[...][None], (rc // SUBLANES, SUBLANES, D_CONV))
        for w in range(CONV_W):
            s = (HIST_OFF + w) % SUBLANES
            a0 = c0 + HIST_OFF + w - s
            src = ext_ref[a0:a0 + rc] if s == 0 else sh_ref[s - 1, a0:a0 + rc]
            acc = acc + src.reshape(rc // SUBLANES, SUBLANES, D_CONV) * dww_ref[w][None]
        acc = acc.reshape(rc, D_CONV)
        mu = jnp.mean(acc, axis=-1, keepdims=True)
        xc = acc - mu
        var = jnp.mean(xc * xc, axis=-1, keepdims=True)
        cn = _silu(xc * lax.rsqrt(var + LN_EPS) * cnw_ref[...] + cnb_ref[...])
        cn_ref[0, c0:c0 + rc] = cn.astype(BF16)

    rc = min(CONV_ROWS, ts)
    for c0 in range(0, ts, rc):
        conv_chunk(c0, rc)

    q_ref[0] = (proj(0) * (-LOG2E * HEAD_DIM ** -0.5)).astype(BF16)
    for c, bf_ref, heads_ref in ((1, k_ref, ko_ref), (2, v_ref, vo_ref)):
        t = proj(c)
        bf_ref[0] = t.astype(BF16)
        for hd in range(N_HEADS):
            heads_ref[0, 0, hd] = t[:, hd * HEAD_DIM:(hd + 1) * HEAD_DIM]
    ga_ref[0] = _silu(proj(3)).astype(BF16)
    gc_ref[0] = _silu(proj(6)).astype(BF16)

    if multi_block:
        ext_ref[0:HIST] = ext_ref[ts:ts + HIST]


def _in_proj(x, norm_w, w_in_bf, hist, dw_w, dw_b, cn_w, cn_b, ts):
    b, s, _ = x.shape
    assert s % ts == 0 and (ts >= HIST or s == ts)
    grid = (b, s // ts)
    row = lambda bi, i: (bi, i, 0)
    const = lambda bi, i: (0, 0)
    heads = lambda bi, i: (0, bi, 0, i, 0)
    bf_spec = pl.BlockSpec((1, ts, D_ATT), row)
    vec = pl.BlockSpec((1, D_CONV), const)
    out_shape = [
        jax.ShapeDtypeStruct((b, s, D_ATT), BF16),
        jax.ShapeDtypeStruct((b, s, D_ATT), BF16),
        jax.ShapeDtypeStruct((b, s, D_ATT), BF16),
        jax.ShapeDtypeStruct((b, s, D_ATT), BF16),
        jax.ShapeDtypeStruct((b, s, D_CONV), BF16),
        jax.ShapeDtypeStruct((b, s, D_CONV), BF16),
        jax.ShapeDtypeStruct((1, b, N_HEADS, s, HEAD_DIM), F32),
        jax.ShapeDtypeStruct((1, b, N_HEADS, s, HEAD_DIM), F32),
        jax.ShapeDtypeStruct((1, b, CONV_W - 1, D_CONV), F32),
    ]
    out_specs = [bf_spec] * 6 + [
        pl.BlockSpec((1, 1, N_HEADS, ts, HEAD_DIM), heads),
        pl.BlockSpec((1, 1, N_HEADS, ts, HEAD_DIM), heads),
        pl.BlockSpec((1, 1, CONV_W - 1, D_CONV), lambda bi, i: (0, bi, 0, 0))]
    return pl.pallas_call(
        functools.partial(_in_proj_kernel, ts=ts, multi_block=s > ts),
        grid=grid,
        in_specs=[pl.BlockSpec((1, ts, D_MODEL), row),
                  pl.BlockSpec((1, D_MODEL), const),
                  pl.BlockSpec((D_MODEL, D_IN), const),
                  pl.BlockSpec((1, HIST, D_CONV), lambda bi, i: (bi, 0, 0)),
                  pl.BlockSpec((CONV_W, SUBLANES, D_CONV), lambda bi, i: (0, 0, 0)),
                  pl.BlockSpec((SUBLANES, D_CONV), const),
                  vec, vec],
        out_specs=out_specs,
        out_shape=out_shape,
        scratch_shapes=[pltpu.VMEM((HIST + ts, D_CONV), F32),
                        pltpu.VMEM((SUBLANES - 1, HIST + ts - SUBLANES, D_CONV), F32)],
        compiler_params=pltpu.CompilerParams(
            dimension_semantics=("arbitrary", "arbitrary"), vmem_limit_bytes=VMEM_LIMIT),
        name="in_proj",
    )(x, norm_w, w_in_bf, hist, dw_w, dw_b, cn_w, cn_b)


def _sb_attn_kernel(q_ref, k_ref, v_ref, ga_ref, tri_ref, o_ref, qm_ref, acc_ref, carry_ref, live_ref,
                    *, tq, tk, q_pos0, n_pair, top):
    i = pl.program_id(2)
    n_head = 2 * n_pair
    q_start = q_pos0 + i * tq
    j_diag = q_start // tk
    lane = lax.broadcasted_iota(jnp.int32, (tq, LANES), 1)
    zero = jnp.zeros((tq, LANES), BF16)
    for p in range(n_pair):
        q2 = q_ref[0, :, p * LANES:(p + 1) * LANES]
        qm_ref[2 * p] = jnp.where(lane < HEAD_DIM, q2, zero)
        qm_ref[2 * p + 1] = jnp.where(lane >= HEAD_DIM, q2, zero)
    acc_ref[...] = jnp.zeros_like(acc_ref)
    carry_ref[...] = jnp.zeros_like(carry_ref)
    rep = tk // LANES
    heads = range(n_head)
    cols = [slice((hh // 2) * LANES, (hh // 2 + 1) * LANES) for hh in heads]

    def tile(j, masked, rows):
        k0 = pl.multiple_of(j * tk, tk)
        kj = k_ref[0, pl.ds(k0, tk), :]
        vj = v_ref[0, pl.ds(k0, tk), :]
        if masked:
            q_pos = q_start + lax.broadcasted_iota(jnp.int32, (rows, tk), 0)
            k_pos = k0 + lax.broadcasted_iota(jnp.int32, (rows, tk), 1)
            valid = k_pos < q_pos
        ws = [lax.dot_general(qm_ref[hh, 0:rows], kj[:, cols[hh]], (((1,), (1,)), ((), ())),
                              preferred_element_type=F32) for hh in heads]
        lfs = []
        for hh in heads:
            w = ws[hh]
            lf = jnp.minimum(w, 0.0) - jnp.log2(1.0 + jnp.exp2(jnp.minimum(w, -w)))
            if masked:
                lf = jnp.where(valid, lf, 0.0)
            lfs.append(lf.astype(BF16))
        css = [jnp.dot(lfs[hh], tri_ref[...], preferred_element_type=F32) for hh in heads]
        ats = []
        worst = None
        for hh in heads:
            t1 = css[hh] + jnp.concatenate([carry_ref[hh, 0:rows]] * rep, axis=1)
            a = jnp.exp2(t1 - ws[hh])
            if masked:
                a = jnp.where(valid, a, 0.0)
            ats.append(a.astype(BF16))
            total = t1[:, 0:1]
            carry_ref[hh, 0:rows] = jnp.broadcast_to(total, (rows, LANES))
            worst = total if worst is None else jnp.maximum(worst, total)
        for hh in heads:
            acc_ref[hh, 0:rows] += jnp.dot(ats[hh], vj[:, cols[hh]], preferred_element_type=F32)
        live_ref[0] = (jnp.max(worst[0:top]) >= DEAD_LOG2).astype(jnp.int32)
        live_ref[1] = (jnp.max(worst[top:rows]) >= DEAD_LOG2).astype(jnp.int32) if rows > top else jnp.int32(0)

    tile(j_diag, True, tq)

    def body(c):
        j, _, live_bottom = c
        if top < tq:
            @pl.when(live_bottom > 0)
            def _():
                tile(j, False, tq)

            @pl.when(live_bottom == 0)
            def _():
                tile(j, False, top)
        else:
            tile(j, False, tq)
        return j - 1, live_ref[0], live_ref[1]

    lax.while_loop(lambda c: jnp.logical_and(c[0] >= 0, c[1] + c[2] > 0), body,
                   (j_diag - 1, live_ref[0], live_ref[1]))
    for p in range(n_pair):
        att = jnp.where(lane < HEAD_DIM, acc_ref[2 * p], acc_ref[2 * p + 1])
        pc = slice(p * LANES, (p + 1) * LANES)
        o_ref[0, :, pc] = (att * ga_ref[0, :, pc].astype(F32)).astype(BF16)


def _tri(tk):
    j = jnp.arange(tk)[:, None]
    s = jnp.arange(tk)[None, :]
    return (j >= s).astype(BF16)


def _sb_attn(q, k, v, ga, tq, tk, q_pos0, n_pair):
    b, sq, _ = q.shape
    lk = k.shape[1]
    assert tk % tq == 0 and q_pos0 % tq == 0 and lk % tk == 0 and q_pos0 + sq <= lk
    width = n_pair * LANES
    top = 3 * tq // 4 if tq % 64 == 0 else tq
    grid = (b, D_ATT // width, sq // tq)
    qspec = pl.BlockSpec((1, tq, width), lambda bi, p, i: (bi, i, p))
    kspec = pl.BlockSpec((1, lk, width), lambda bi, p, i: (bi, 0, p))
    return pl.pallas_call(
        functools.partial(_sb_attn_kernel, tq=tq, tk=tk, q_pos0=q_pos0, n_pair=n_pair, top=top),
        grid=grid,
        in_specs=[qspec, kspec, kspec, qspec,
                  pl.BlockSpec((tk, tk), lambda bi, p, i: (0, 0))],
        out_specs=qspec,
        out_shape=jax.ShapeDtypeStruct((b, sq, D_ATT), BF16),
        scratch_shapes=[pltpu.VMEM((2 * n_pair, tq, LANES), BF16),
                        pltpu.VMEM((2 * n_pair, tq, LANES), F32),
                        pltpu.VMEM((2 * n_pair, tq, LANES), F32),
                        pltpu.SMEM((2,), jnp.int32)],
        compiler_params=pltpu.CompilerParams(
            dimension_semantics=("arbitrary", "arbitrary", "arbitrary"), vmem_limit_bytes=VMEM_LIMIT),
        name="sb_attn",
    )(q, k, v, ga, _tri(tk))


def _tail_kernel(x_ref, ma_ref, cn_ref, gc_ref, wpw_ref, bpw_ref, wout_ref, fnw_ref, y_ref):
    conv = jnp.dot(cn_ref[0], wpw_ref[...], preferred_element_type=F32) + bpw_ref[...]
    mc = (conv * gc_ref[0].astype(F32)).astype(BF16)
    mix = (jnp.dot(ma_ref[0], wout_ref[0:D_ATT], preferred_element_type=F32)
           + jnp.dot(mc, wout_ref[D_ATT:], preferred_element_type=F32))
    xn = x_ref[0] + mix
    r = lax.rsqrt(jnp.mean(xn * xn, axis=-1, keepdims=True) + EPS)
    y_ref[0] = xn * r * fnw_ref[...]


def _tail(x, ma, cn, gc, w_pw_bf, b_pw, w_out_bf, fn_w, ts):
    b, s, _ = x.shape
    assert s % ts == 0
    row = lambda bi, i: (bi, i, 0)
    const = lambda bi, i: (0, 0)
    half = pl.BlockSpec((1, ts, D_ATT), row)
    return pl.pallas_call(
        _tail_kernel,
        grid=(b, s // ts),
        in_specs=[pl.BlockSpec((1, ts, D_MODEL), row), half, half, half,
                  pl.BlockSpec((D_CONV, D_CONV), const),
                  pl.BlockSpec((1, D_CONV), const),
                  pl.BlockSpec((D_MODEL, D_MODEL), const),
                  pl.BlockSpec((1, D_MODEL), const)],
        out_specs=pl.BlockSpec((1, ts, D_MODEL), row),
        out_shape=jax.ShapeDtypeStruct((b, s, D_MODEL), F32),
        compiler_params=pltpu.CompilerParams(
            dimension_semantics=("arbitrary", "arbitrary"), vmem_limit_bytes=VMEM_LIMIT),
        name="tail",
    )(x, ma, cn, gc, w_pw_bf, b_pw, w_out_bf, fn_w)


def _pad_rows(t, n):
    return jnp.pad(t, ((0, 0), (0, n - t.shape[1]), (0, 0)))


def _from_heads_bf(t):
    b, h, s, d = t.shape
    return t.astype(BF16).transpose(0, 2, 1, 3).reshape(b, s, h * d)


def kernel(x_prompt, x_sample, cache_k, cache_v, state_conv, norm_w, w_in, dw_w, dw_b, cn_w, cn_b,
           w_pw, b_pw, w_out, final_norm_w):
    depth = w_in.shape[0]
    assert depth == 1
    l = 0
    b, s, _ = x_prompt.shape
    db, n_new, _ = x_sample.shape
    past = cache_k.shape[3]

    w_in_bf = w_in[l].astype(BF16)
    conv_w = (jnp.broadcast_to(dw_w[l][:, None, :], (CONV_W, SUBLANES, D_CONV)),
              jnp.broadcast_to(dw_b[l][None], (SUBLANES, D_CONV)), cn_w[l][None], cn_b[l][None])
    tail_w = (w_pw[l].astype(BF16), b_pw[l][None], w_out[l].astype(BF16), final_norm_w[None])
    nw = norm_w[l][None]

    tq_p = min(256, s)
    hist0 = jnp.zeros((b, HIST, D_CONV), F32)
    q, k, v, ga, cn, gc, kp, vp, cp = _in_proj(x_prompt, nw, w_in_bf, hist0, *conv_w, ts=min(512, s))
    ma = _sb_attn(q, k, v, ga, tq_p, tq_p, 0, n_pair=4)
    y_p = _tail(x_prompt, ma, cn, gc, *tail_w, ts=min(1024, s))

    tk_s = 256
    hist_s = jnp.pad(state_conv[l], ((0, 0), (HIST_OFF, 0), (0, 0)))
    q, k, v, ga, cn, gc, ks, vs, cs = _in_proj(x_sample, nw, w_in_bf, hist_s, *conv_w, ts=n_new)
    lk = -(-(past + n_new) // tk_s) * tk_s
    k_all = _pad_rows(jnp.concatenate([_from_heads_bf(cache_k[l]), k], axis=1), lk)
    v_all = _pad_rows(jnp.concatenate([_from_heads_bf(cache_v[l]), v], axis=1), lk)
    ma = _sb_attn(q, k_all, v_all, ga, n_new, tk_s, past, n_pair=2)
    y_s = _tail(x_sample, ma, cn, gc, *tail_w, ts=n_new)

    return (y_p, y_s, kp, vp, cp, ks, vs, cs)
```

```python
import functools

import jax
import jax.numpy as jnp
from jax import lax
from jax.experimental import pallas as pl
from jax.experimental.pallas import tpu as pltpu

D_MODEL = 1024
D_ATT = 512
D_CONV = 512
HEAD_DIM = 64
N_HEADS = D_ATT // HEAD_DIM
CONV_W = 31
SUBLANES = 8
LANES = 128
HIST = 32
HIST_OFF = HIST - (CONV_W - 1)
CONV_ROWS = 64
EPS = 1e-6
LN_EPS = 1e-5
D_IN = 4 * D_ATT + 3 * D_CONV
LOG2E = 1.4426950408889634
DEAD_LOG2 = -170.0
VMEM_LIMIT = 56 * 1024 * 1024

BF16 = jnp.bfloat16
F32 = jnp.float32


def _sigmoid(x):
    return 1.0 / (1.0 + jnp.exp(-x))


def _silu(x):
    return x * _sigmoid(x)


def _in_proj_kernel(x_ref, nw_ref, w_ref, hist_ref, dww_ref, dwb_ref, cnw_ref, cnb_ref,
                    q_ref, k_ref, v_ref, ga_ref, cn_ref, gc_ref, ko_ref, vo_ref, cs_ref,
                    ext_ref, sh_ref, *, ts, multi_block):
    i = pl.program_id(1)
    x = x_ref[0]
    r = lax.rsqrt(jnp.mean(x * x, axis=-1, keepdims=True) + EPS)
    h = (x * r * nw_ref[...]).astype(BF16)

    def proj(c):
        return jnp.dot(h, w_ref[:, c * D_ATT:(c + 1) * D_ATT], preferred_element_type=F32)

    @pl.when(i == 0)
    def _():
        ext_ref[0:HIST] = hist_ref[0]

    ext_ref[HIST:HIST + ts] = proj(4) * _sigmoid(proj(5))

    @pl.when(i == pl.num_programs(1) - 1)
    def _():
        cs_ref[0, 0] = ext_ref[ts + HIST_OFF:ts + HIST]

    n_sh = ts + HIST - SUBLANES
    for s in range(1, SUBLANES):
        sh_ref[s - 1] = ext_ref[s:s + n_sh]

    def conv_chunk(c0, rc):
        acc = jnp.broadcast_to(dwb_ref[...][None], (rc // SUBLANES, SUBLANES, D_CONV))
        for w in range(CONV_W):
            s = (HIST_OFF + w) % SUBLANES
            a0 = c0 + HIST_OFF + w - s
            src = ext_ref[a0:a0 + rc] if s == 0 else sh_ref[s - 1, a0:a0 + rc]
            acc = acc + src.reshape(rc // SUBLANES, SUBLANES, D_CONV) * dww_ref[w][None]
        acc = acc.reshape(rc, D_CONV)
        mu = jnp.mean(acc, axis=-1, keepdims=True)
        xc = acc - mu
        var = jnp.mean(xc * xc, axis=-1, keepdims=True)
        cn = _silu(xc * lax.rsqrt(var + LN_EPS) * cnw_ref[...] + cnb_ref[...])
        cn_ref[0, c0:c0 + rc] = cn.astype(BF16)

    rc = min(CONV_ROWS, ts)
    for c0 in range(0, ts, rc):
        conv_chunk(c0, rc)

    q_ref[0] = (proj(0) * (-LOG2E * HEAD_DIM ** -0.5)).astype(BF16)
    for c, bf_ref, heads_ref in ((1, k_ref, ko_ref), (2, v_ref, vo_ref)):
        t = proj(c)
        bf_ref[0] = t.astype(BF16)
        for hd in range(N_HEADS):
            heads_ref[0, 0, hd] = t[:, hd * HEAD_DIM:(hd + 1) * HEAD_DIM]
    ga_ref[0] = _silu(proj(3)).astype(BF16)
    gc_ref[0] = _silu(proj(6)).astype(BF16)

    if multi_block:
        ext_ref[0:HIST] = ext_ref[ts:ts + HIST]


def _in_proj(x, norm_w, w_in_bf, hist, dw_w, dw_b, cn_w, cn_b, ts):
    b, s, _ = x.shape
    assert s % ts == 0 and (ts >= HIST or s == ts)
    grid = (b, s // ts)
    row = lambda bi, i: (bi, i, 0)
    const = lambda bi, i: (0, 0)
    heads = lambda bi, i: (0, bi, 0, i, 0)
    bf_spec = pl.BlockSpec((1, ts, D_ATT), row)
    vec = pl.BlockSpec((1, D_CONV), const)
    out_shape = [
        jax.ShapeDtypeStruct((b, s, D_ATT), BF16),
        jax.ShapeDtypeStruct((b, s, D_ATT), BF16),
        jax.ShapeDtypeStruct((b, s, D_ATT), BF16),
        jax.ShapeDtypeStruct((b, s, D_ATT), BF16),
        jax.ShapeDtypeStruct((b, s, D_CONV), BF16),
        jax.ShapeDtypeStruct((b, s, D_CONV), BF16),
        jax.ShapeDtypeStruct((1, b, N_HEADS, s, HEAD_DIM), F32),
        jax.ShapeDtypeStruct((1, b, N_HEADS, s, HEAD_DIM), F32),
        jax.ShapeDtypeStruct((1, b, CONV_W - 1, D_CONV), F32),
    ]
    out_specs = [bf_spec] * 6 + [
        pl.BlockSpec((1, 1, N_HEADS, ts, HEAD_DIM), heads),
        pl.BlockSpec((1, 1, N_HEADS, ts, HEAD_DIM), heads),
        pl.BlockSpec((1, 1, CONV_W - 1, D_CONV), lambda bi, i: (0, bi, 0, 0))]
    return pl.pallas_call(
        functools.partial(_in_proj_kernel, ts=ts, multi_block=s > ts),
        grid=grid,
        in_specs=[pl.BlockSpec((1, ts, D_MODEL), row),
                  pl.BlockSpec((1, D_MODEL), const),
                  pl.BlockSpec((D_MODEL, D_IN), const),
                  pl.BlockSpec((1, HIST, D_CONV), lambda bi, i: (bi, 0, 0)),
                  pl.BlockSpec((CONV_W, SUBLANES, D_CONV), lambda bi, i: (0, 0, 0)),
                  pl.BlockSpec((SUBLANES, D_CONV), const),
                  vec, vec],
        out_specs=out_specs,
        out_shape=out_shape,
        scratch_shapes=[pltpu.VMEM((HIST + ts, D_CONV), F32),
                        pltpu.VMEM((SUBLANES - 1, HIST + ts - SUBLANES, D_CONV), F32)],
        compiler_params=pltpu.CompilerParams(
            dimension_semantics=("arbitrary", "arbitrary"), vmem_limit_bytes=VMEM_LIMIT),
        name="in_proj",
    )(x, norm_w, w_in_bf, hist, dw_w, dw_b, cn_w, cn_b)


def _sb_attn_kernel(q_ref, k_ref, v_ref, ga_ref, tri_ref, o_ref, qm_ref, acc_ref, carry_ref, live_ref,
                    *, tq, tk, q_pos0, n_pair, top):
    i = pl.program_id(2)
    n_head = 2 * n_pair
    q_start = q_pos0 + i * tq
    j_diag = q_start // tk
    lane = lax.broadcasted_iota(jnp.int32, (tq, LANES), 1)
    zero = jnp.zeros((tq, LANES), BF16)
    for p in range(n_pair):
        q2 = q_ref[0, :, p * LANES:(p + 1) * LANES]
        qm_ref[2 * p] = jnp.where(lane < HEAD_DIM, q2, zero)
        qm_ref[2 * p + 1] = jnp.where(lane >= HEAD_DIM, q2, zero)
    acc_ref[...] = jnp.zeros_like(acc_ref)
    carry_ref[...] = jnp.zeros_like(carry_ref)
    heads = range(n_head)
    cols = [slice((hh // 2) * LANES, (hh // 2 + 1) * LANES) for hh in heads]

    def tile(j, masked, r0, r1, kw, merge=False):
        rows = r1 - r0
        k0 = pl.multiple_of(j * tk, tk)
        kj = k_ref[0, pl.ds(k0, kw), :]
        vj = v_ref[0, pl.ds(k0, kw), :]
        if masked:
            q_pos = q_start + r0 + lax.broadcasted_iota(jnp.int32, (rows, kw), 0)
            k_pos = k0 + lax.broadcasted_iota(jnp.int32, (rows, kw), 1)
            valid = k_pos < q_pos
        ws = [lax.dot_general(qm_ref[hh, r0:r1], kj[:, cols[hh]], (((1,), (1,)), ((), ())),
                              preferred_element_type=F32) for hh in heads]
        lfs = []
        for hh in heads:
            w = ws[hh]
            lf = jnp.minimum(w, 0.0) - jnp.log2(1.0 + jnp.exp2(jnp.minimum(w, -w)))
            if masked:
                lf = jnp.where(valid, lf, 0.0)
            lfs.append(lf.astype(BF16))
        css = [jnp.dot(lfs[hh], tri_ref[0:kw, 0:kw], preferred_element_type=F32) for hh in heads]
        ats = []
        worst = None
        for hh in heads:
            t1 = css[hh] + jnp.concatenate([carry_ref[hh, r0:r1]] * (kw // LANES), axis=1)
            a = jnp.exp2(t1 - ws[hh])
            if masked:
                a = jnp.where(valid, a, 0.0)
            ats.append(a.astype(BF16))
            total = t1[:, 0:1]
            carry_ref[hh, r0:r1] = jnp.broadcast_to(total, (rows, LANES))
            worst = total if worst is None else jnp.maximum(worst, total)
        for hh in heads:
            acc_ref[hh, r0:r1] += jnp.dot(ats[hh], vj[:, cols[hh]], preferred_element_type=F32)
        alive = lambda lo, hi: (jnp.max(worst[lo - r0:hi - r0]) >= DEAD_LOG2).astype(jnp.int32)
        if r0 < top:
            low = alive(r0, min(r1, top))
            live_ref[0] = jnp.maximum(live_ref[0], low) if merge else low
        if not merge:
            live_ref[1] = jnp.int32(0)
        if r1 > top:
            live_ref[1] = alive(max(r0, top), r1)

    half = tq // 2
    if tq == tk and half % LANES == 0 and half <= top:
        tile(j_diag, True, 0, half, half)
        tile(j_diag, True, half, tq, tk, merge=True)
    else:
        tile(j_diag, True, 0, tq, tk)

    def body(c):
        j, _, live_bottom = c
        if top < tq:
            @pl.when(live_bottom > 0)
            def _():
                tile(j, False, 0, tq, tk)

            @pl.when(live_bottom == 0)
            def _():
                tile(j, False, 0, top, tk)
        else:
            tile(j, False, 0, tq, tk)
        return j - 1, live_ref[0], live_ref[1]

    lax.while_loop(lambda c: jnp.logical_and(c[0] >= 0, c[1] + c[2] > 0), body,
                   (j_diag - 1, live_ref[0], live_ref[1]))
    for p in range(n_pair):
        att = jnp.where(lane < HEAD_DIM, acc_ref[2 * p], acc_ref[2 * p + 1])
        pc = slice(p * LANES, (p + 1) * LANES)
        o_ref[0, :, pc] = (att * ga_ref[0, :, pc].astype(F32)).astype(BF16)


def _tri(tk):
    j = jnp.arange(tk)[:, None]
    s = jnp.arange(tk)[None, :]
    return (j >= s).astype(BF16)


def _sb_attn(q, k, v, ga, tq, tk, q_pos0, n_pair):
    b, sq, _ = q.shape
    lk = k.shape[1]
    assert tk % tq == 0 and q_pos0 % tq == 0 and lk % tk == 0 and q_pos0 + sq <= lk
    width = n_pair * LANES
    top = 3 * tq // 4 if tq % 64 == 0 else tq
    grid = (b, D_ATT // width, sq // tq)
    qspec = pl.BlockSpec((1, tq, width), lambda bi, p, i: (bi, i, p))
    kspec = pl.BlockSpec((1, lk, width), lambda bi, p, i: (bi, 0, p))
    return pl.pallas_call(
        functools.partial(_sb_attn_kernel, tq=tq, tk=tk, q_pos0=q_pos0, n_pair=n_pair, top=top),
        grid=grid,
        in_specs=[qspec, kspec, kspec, qspec,
                  pl.BlockSpec((tk, tk), lambda bi, p, i: (0, 0))],
        out_specs=qspec,
        out_shape=jax.ShapeDtypeStruct((b, sq, D_ATT), BF16),
        scratch_shapes=[pltpu.VMEM((2 * n_pair, tq, LANES), BF16),
                        pltpu.VMEM((2 * n_pair, tq, LANES), F32),
                        pltpu.VMEM((2 * n_pair, tq, LANES), F32),
                        pltpu.SMEM((2,), jnp.int32)],
        compiler_params=pltpu.CompilerParams(
            dimension_semantics=("arbitrary", "arbitrary", "arbitrary"), vmem_limit_bytes=VMEM_LIMIT),
        name="sb_attn",
    )(q, k, v, ga, _tri(tk))


def _sb_decode_kernel(q_ref, kn_ref, vn_ref, ga_ref, ck_ref, cv_ref, tri_new_ref, tri_past_ref, o_ref,
                      *, n_new, past, tkc):
    q = q_ref[0]
    pad = jnp.zeros((LANES - n_new, D_ATT), BF16)
    kn = jnp.concatenate([kn_ref[0], pad], axis=0)
    vn = jnp.concatenate([vn_ref[0], pad], axis=0)
    t_idx = lax.broadcasted_iota(jnp.int32, (n_new, LANES), 0)
    s_idx = lax.broadcasted_iota(jnp.int32, (n_new, LANES), 1)
    valid = s_idx < t_idx
    nt = (((1,), (1,)), ((), ()))

    def log2_fail(w):
        return jnp.minimum(w, 0.0) - jnp.log2(1.0 + jnp.exp2(jnp.minimum(w, -w)))

    heads = range(N_HEADS)
    hs = [slice(hd * HEAD_DIM, (hd + 1) * HEAD_DIM) for hd in heads]
    blocks = [slice(jb * tkc, (jb + 1) * tkc) for jb in range(past // tkc)]
    wn = [lax.dot_general(q[:, hs[hd]], kn[:, hs[hd]], nt, preferred_element_type=F32) for hd in heads]
    wc = [lax.dot_general(q[:, hs[hd]], ck_ref[0, 0, hd].astype(BF16), nt, preferred_element_type=F32)
          for hd in heads]
    lfn = [jnp.where(valid, log2_fail(wn[hd]), 0.0).astype(BF16) for hd in heads]
    lfc = [log2_fail(wc[hd]).astype(BF16) for hd in heads]
    csn = [jnp.dot(lfn[hd], tri_new_ref[...], preferred_element_type=F32) for hd in heads]
    csc = [[jnp.dot(lfc[hd][:, ks], tri_past_ref[...], preferred_element_type=F32) for ks in blocks]
           for hd in heads]
    an, ac = [], []
    for hd in heads:
        an.append(jnp.where(valid, jnp.exp2(csn[hd] - wn[hd]), 0.0).astype(BF16))
        carry = csn[hd][:, 0:1]
        parts = [None] * len(blocks)
        for jb in reversed(range(len(blocks))):
            t1 = csc[hd][jb] + carry
            parts[jb] = jnp.exp2(t1 - wc[hd][:, blocks[jb]]).astype(BF16)
            carry = t1[:, 0:1]
        ac.append(jnp.concatenate(parts, axis=1))
    outs = [jnp.dot(an[hd], vn[:, hs[hd]], preferred_element_type=F32)
            + jnp.dot(ac[hd], cv_ref[0, 0, hd].astype(BF16), preferred_element_type=F32) for hd in heads]
    att = jnp.concatenate(outs, axis=1)
    o_ref[0] = (att * ga_ref[0].astype(F32)).astype(BF16)


def _sb_decode(q, k_new, v_new, ga, cache_k, cache_v):
    b, n_new, _ = q.shape
    past = cache_k.shape[3]
    tkc = 256 if past % 256 == 0 else LANES
    assert n_new <= LANES and past % tkc == 0
    row = pl.BlockSpec((1, n_new, D_ATT), lambda bi: (bi, 0, 0))
    cache = pl.BlockSpec((1, 1, N_HEADS, past, HEAD_DIM), lambda bi: (0, bi, 0, 0, 0))
    return pl.pallas_call(
        functools.partial(_sb_decode_kernel, n_new=n_new, past=past, tkc=tkc),
        grid=(b,),
        in_specs=[row, row, row, row, cache, cache,
                  pl.BlockSpec((LANES, LANES), lambda bi: (0, 0)),
                  pl.BlockSpec((tkc, tkc), lambda bi: (0, 0))],
        out_specs=row,
        out_shape=jax.ShapeDtypeStruct((b, n_new, D_ATT), BF16),
        compiler_params=pltpu.CompilerParams(dimension_semantics=("arbitrary",), vmem_limit_bytes=VMEM_LIMIT),
        name="sb_decode",
    )(q, k_new, v_new, ga, cache_k, cache_v, _tri(LANES), _tri(tkc))


def _tail_kernel(x_ref, ma_ref, cn_ref, gc_ref, wpw_ref, bpw_ref, wout_ref, fnw_ref, y_ref):
    conv = jnp.dot(cn_ref[0], wpw_ref[...], preferred_element_type=F32) + bpw_ref[...]
    mc = (conv * gc_ref[0].astype(F32)).astype(BF16)
    mix = (jnp.dot(ma_ref[0], wout_ref[0:D_ATT], preferred_element_type=F32)
           + jnp.dot(mc, wout_ref[D_ATT:], preferred_element_type=F32))
    xn = x_ref[0] + mix
    r = lax.rsqrt(jnp.mean(xn * xn, axis=-1, keepdims=True) + EPS)
    y_ref[0] = xn * r * fnw_ref[...]


def _tail(x, ma, cn, gc, w_pw_bf, b_pw, w_out_bf, fn_w, ts):
    b, s, _ = x.shape
    assert s % ts == 0
    row = lambda bi, i: (bi, i, 0)
    const = lambda bi, i: (0, 0)
    half = pl.BlockSpec((1, ts, D_ATT), row)
    return pl.pallas_call(
        _tail_kernel,
        grid=(b, s // ts),
        in_specs=[pl.BlockSpec((1, ts, D_MODEL), row), half, half, half,
                  pl.BlockSpec((D_CONV, D_CONV), const),
                  pl.BlockSpec((1, D_CONV), const),
                  pl.BlockSpec((D_MODEL, D_MODEL), const),
                  pl.BlockSpec((1, D_MODEL), const)],
        out_specs=pl.BlockSpec((1, ts, D_MODEL), row),
        out_shape=jax.ShapeDtypeStruct((b, s, D_MODEL), F32),
        compiler_params=pltpu.CompilerParams(
            dimension_semantics=("arbitrary", "arbitrary"), vmem_limit_bytes=VMEM_LIMIT),
        name="tail",
    )(x, ma, cn, gc, w_pw_bf, b_pw, w_out_bf, fn_w)


def kernel(x_prompt, x_sample, cache_k, cache_v, state_conv, norm_w, w_in, dw_w, dw_b, cn_w, cn_b,
           w_pw, b_pw, w_out, final_norm_w):
    depth = w_in.shape[0]
    assert depth == 1
    l = 0
    b, s, _ = x_prompt.shape
    db, n_new, _ = x_sample.shape
    past = cache_k.shape[3]

    w_in_bf = w_in[l].astype(BF16)
    conv_w = (jnp.broadcast_to(dw_w[l][:, None, :], (CONV_W, SUBLANES, D_CONV)),
              jnp.broadcast_to(dw_b[l][None], (SUBLANES, D_CONV)), cn_w[l][None], cn_b[l][None])
    tail_w = (w_pw[l].astype(BF16), b_pw[l][None], w_out[l].astype(BF16), final_norm_w[None])
    nw = norm_w[l][None]

    tq_p = min(256, s)
    hist0 = jnp.zeros((b, HIST, D_CONV), F32)
    q, k, v, ga, cn, gc, kp, vp, cp = _in_proj(x_prompt, nw, w_in_bf, hist0, *conv_w, ts=min(512, s))
    ma = _sb_attn(q, k, v, ga, tq_p, tq_p, 0, n_pair=4)
    y_p = _tail(x_prompt, ma, cn, gc, *tail_w, ts=min(1024, s))

    hist_s = jnp.pad(state_conv[l], ((0, 0), (HIST_OFF, 0), (0, 0)))
    q, k, v, ga, cn, gc, ks, vs, cs = _in_proj(x_sample, nw, w_in_bf, hist_s, *conv_w, ts=n_new)
    ma = _sb_decode(q, k, v, ga, cache_k, cache_v)
    y_s = _tail(x_sample, ma, cn, gc, *tail_w, ts=n_new)

    return (y_p, y_s, kp, vp, cp, ks, vs, cs)
```

```python
import functools

import jax
import jax.numpy as jnp
from jax import lax
from jax.experimental import pallas as pl
from jax.experimental.pallas import tpu as pltpu

D_MODEL = 1024
D_ATT = 512
D_CONV = 512
HEAD_DIM = 64
N_HEADS = D_ATT // HEAD_DIM
CONV_W = 31
SUBLANES = 8
LANES = 128
HIST = 32
HIST_OFF = HIST - (CONV_W - 1)
CONV_ROWS = 64
EPS = 1e-6
LN_EPS = 1e-5
D_IN = 4 * D_ATT + 3 * D_CONV
LOG2E = 1.4426950408889634
DEAD_LOG2 = -170.0
VMEM_LIMIT = 56 * 1024 * 1024

BF16 = jnp.bfloat16
F32 = jnp.float32


def _sigmoid(x):
    return 1.0 / (1.0 + jnp.exp(-x))


def _silu(x):
    return x * _sigmoid(x)


def _in_proj_kernel(x_ref, nw_ref, w_ref, hist_ref, dww_ref, dwb_ref, cnw_ref, cnb_ref,
                    q_ref, k_ref, v_ref, ga_ref, cn_ref, gc_ref, ko_ref, vo_ref, cs_ref,
                    ext_ref, sh_ref, *, ts, multi_block):
    i = pl.program_id(1)
    x = x_ref[0]
    r = lax.rsqrt(jnp.mean(x * x, axis=-1, keepdims=True) + EPS)
    h = (x * r * nw_ref[...]).astype(BF16)

    def proj(c):
        return jnp.dot(h, w_ref[:, c * D_ATT:(c + 1) * D_ATT], preferred_element_type=F32)

    @pl.when(i == 0)
    def _():
        ext_ref[0:HIST] = hist_ref[0]

    ext_ref[HIST:HIST + ts] = proj(4) * _sigmoid(proj(5))

    @pl.when(i == pl.num_programs(1) - 1)
    def _():
        cs_ref[0, 0] = ext_ref[ts + HIST_OFF:ts + HIST]

    n_sh = ts + HIST - SUBLANES
    for s in range(1, SUBLANES):
        sh_ref[s - 1] = ext_ref[s:s + n_sh]

    def conv_chunk(c0, rc):
        acc = jnp.broadcast_to(dwb_ref[...][None], (rc // SUBLANES, SUBLANES, D_CONV))
        for w in range(CONV_W):
            s = (HIST_OFF + w) % SUBLANES
            a0 = c0 + HIST_OFF + w - s
            src = ext_ref[a0:a0 + rc] if s == 0 else sh_ref[s - 1, a0:a0 + rc]
            acc = acc + src.reshape(rc // SUBLANES, SUBLANES, D_CONV) * dww_ref[w][None]
        acc = acc.reshape(rc, D_CONV)
        mu = jnp.mean(acc, axis=-1, keepdims=True)
        xc = acc - mu
        var = jnp.mean(xc * xc, axis=-1, keepdims=True)
        cn = _silu(xc * lax.rsqrt(var + LN_EPS) * cnw_ref[...] + cnb_ref[...])
        cn_ref[0, c0:c0 + rc] = cn.astype(BF16)

    rc = min(CONV_ROWS, ts)
    for c0 in range(0, ts, rc):
        conv_chunk(c0, rc)

    q_ref[0] = (proj(0) * (-LOG2E * HEAD_DIM ** -0.5)).astype(BF16)
    for c, bf_ref, heads_ref in ((1, k_ref, ko_ref), (2, v_ref, vo_ref)):
        t = proj(c)
        bf_ref[0] = t.astype(BF16)
        for hd in range(N_HEADS):
            heads_ref[0, 0, hd] = t[:, hd * HEAD_DIM:(hd + 1) * HEAD_DIM]
    ga_ref[0] = _silu(proj(3)).astype(BF16)
    gc_ref[0] = _silu(proj(6)).astype(BF16)

    if multi_block:
        ext_ref[0:HIST] = ext_ref[ts:ts + HIST]


def _in_proj(x, norm_w, w_in_bf, hist, dw_w, dw_b, cn_w, cn_b, ts):
    b, s, _ = x.shape
    assert s % ts == 0 and (ts >= HIST or s == ts)
    grid = (b, s // ts)
    row = lambda bi, i: (bi, i, 0)
    const = lambda bi, i: (0, 0)
    heads = lambda bi, i: (0, bi, 0, i, 0)
    bf_spec = pl.BlockSpec((1, ts, D_ATT), row)
    vec = pl.BlockSpec((1, D_CONV), const)
    out_shape = [
        jax.ShapeDtypeStruct((b, s, D_ATT), BF16),
        jax.ShapeDtypeStruct((b, s, D_ATT), BF16),
        jax.ShapeDtypeStruct((b, s, D_ATT), BF16),
        jax.ShapeDtypeStruct((b, s, D_ATT), BF16),
        jax.ShapeDtypeStruct((b, s, D_CONV), BF16),
        jax.ShapeDtypeStruct((b, s, D_CONV), BF16),
        jax.ShapeDtypeStruct((1, b, N_HEADS, s, HEAD_DIM), F32),
        jax.ShapeDtypeStruct((1, b, N_HEADS, s, HEAD_DIM), F32),
        jax.ShapeDtypeStruct((1, b, CONV_W - 1, D_CONV), F32),
    ]
    out_specs = [bf_spec] * 6 + [
        pl.BlockSpec((1, 1, N_HEADS, ts, HEAD_DIM), heads),
        pl.BlockSpec((1, 1, N_HEADS, ts, HEAD_DIM), heads),
        pl.BlockSpec((1, 1, CONV_W - 1, D_CONV), lambda bi, i: (0, bi, 0, 0))]
    return pl.pallas_call(
        functools.partial(_in_proj_kernel, ts=ts, multi_block=s > ts),
        grid=grid,
        in_specs=[pl.BlockSpec((1, ts, D_MODEL), row),
                  pl.BlockSpec((1, D_MODEL), const),
                  pl.BlockSpec((D_MODEL, D_IN), const),
                  pl.BlockSpec((1, HIST, D_CONV), lambda bi, i: (bi, 0, 0)),
                  pl.BlockSpec((CONV_W, SUBLANES, D_CONV), lambda bi, i: (0, 0, 0)),
                  pl.BlockSpec((SUBLANES, D_CONV), const),
                  vec, vec],
        out_specs=out_specs,
        out_shape=out_shape,
        scratch_shapes=[pltpu.VMEM((HIST + ts, D_CONV), F32),
                        pltpu.VMEM((SUBLANES - 1, HIST + ts - SUBLANES, D_CONV), F32)],
        compiler_params=pltpu.CompilerParams(
            dimension_semantics=("arbitrary", "arbitrary"), vmem_limit_bytes=VMEM_LIMIT),
        name="in_proj",
    )(x, norm_w, w_in_bf, hist, dw_w, dw_b, cn_w, cn_b)


def _sb_attn_kernel(q_ref, k_ref, v_ref, ga_ref, tri_ref, o_ref, qm_ref, acc_ref, carry_ref, live_ref,
                    *, tq, tk, q_pos0, n_pair, top):
    i = pl.program_id(2)
    n_head = 2 * n_pair
    q_start = q_pos0 + i * tq
    j_diag = q_start // tk
    lane = lax.broadcasted_iota(jnp.int32, (tq, LANES), 1)
    zero = jnp.zeros((tq, LANES), BF16)
    for p in range(n_pair):
        q2 = q_ref[0, :, p * LANES:(p + 1) * LANES]
        qm_ref[2 * p] = jnp.where(lane < HEAD_DIM, q2, zero)
        qm_ref[2 * p + 1] = jnp.where(lane >= HEAD_DIM, q2, zero)
    acc_ref[...] = jnp.zeros_like(acc_ref)
    carry_ref[...] = jnp.zeros_like(carry_ref)
    heads = range(n_head)
    cols = [slice((hh // 2) * LANES, (hh // 2 + 1) * LANES) for hh in heads]

    def tile(j, masked, r0, r1, kw, merge=False):
        rows = r1 - r0
        k0 = pl.multiple_of(j * tk, tk)
        kj = k_ref[0, pl.ds(k0, kw), :]
        vj = v_ref[0, pl.ds(k0, kw), :]
        if masked:
            q_pos = q_start + r0 + lax.broadcasted_iota(jnp.int32, (rows, kw), 0)
            k_pos = k0 + lax.broadcasted_iota(jnp.int32, (rows, kw), 1)
            valid = k_pos < q_pos
        ws = [lax.dot_general(qm_ref[hh, r0:r1], kj[:, cols[hh]], (((1,), (1,)), ((), ())),
                              preferred_element_type=F32) for hh in heads]
        lfs = []
        for hh in heads:
            w = ws[hh]
            lf = jnp.minimum(w, 0.0) - jnp.log2(1.0 + jnp.exp2(jnp.minimum(w, -w)))
            if masked:
                lf = jnp.where(valid, lf, 0.0)
            lfs.append(lf.astype(BF16))
        css = [jnp.dot(lfs[hh], tri_ref[0:kw, 0:kw], preferred_element_type=F32) for hh in heads]
        ats = []
        worst = None
        for hh in heads:
            t1 = css[hh] + jnp.concatenate([carry_ref[hh, r0:r1]] * (kw // LANES), axis=1)
            a = jnp.exp2(t1 - ws[hh])
            if masked:
                a = jnp.where(valid, a, 0.0)
            ats.append(a.astype(BF16))
            total = t1[:, 0:1]
            carry_ref[hh, r0:r1] = jnp.broadcast_to(total, (rows, LANES))
            worst = total if worst is None else jnp.maximum(worst, total)
        for hh in heads:
            acc_ref[hh, r0:r1] += jnp.dot(ats[hh], vj[:, cols[hh]], preferred_element_type=F32)
        alive = lambda lo, hi: (jnp.max(worst[lo - r0:hi - r0]) >= DEAD_LOG2).astype(jnp.int32)
        if r0 < top:
            low = alive(r0, min(r1, top))
            live_ref[0] = jnp.maximum(live_ref[0], low) if merge else low
        if not merge:
            live_ref[1] = jnp.int32(0)
        if r1 > top:
            live_ref[1] = alive(max(r0, top), r1)

    half = tq // 2
    if tq == tk and half % LANES == 0 and half <= top:
        tile(j_diag, True, 0, half, half)
        tile(j_diag, True, half, tq, tk, merge=True)
    else:
        tile(j_diag, True, 0, tq, tk)

    def body(c):
        j, _, live_bottom = c
        if top < tq:
            @pl.when(live_bottom > 0)
            def _():
                tile(j, False, 0, tq, tk)

            @pl.when(live_bottom == 0)
            def _():
                tile(j, False, 0, top, tk)
        else:
            tile(j, False, 0, tq, tk)
        return j - 1, live_ref[0], live_ref[1]

    lax.while_loop(lambda c: jnp.logical_and(c[0] >= 0, c[1] + c[2] > 0), body,
                   (j_diag - 1, live_ref[0], live_ref[1]))
    for p in range(n_pair):
        att = jnp.where(lane < HEAD_DIM, acc_ref[2 * p], acc_ref[2 * p + 1])
        pc = slice(p * LANES, (p + 1) * LANES)
        o_ref[0, :, pc] = (att * ga_ref[0, :, pc].astype(F32)).astype(BF16)


def _tri(tk):
    j = jnp.arange(tk)[:, None]
    s = jnp.arange(tk)[None, :]
    return (j >= s).astype(BF16)


def _sb_attn(q, k, v, ga, tq, tk, q_pos0, n_pair):
    b, sq, _ = q.shape
    lk = k.shape[1]
    assert tk % tq == 0 and q_pos0 % tq == 0 and lk % tk == 0 and q_pos0 + sq <= lk
    width = n_pair * LANES
    top = 3 * tq // 4 if tq % 64 == 0 else tq
    grid = (b, D_ATT // width, sq // tq)
    qspec = pl.BlockSpec((1, tq, width), lambda bi, p, i: (bi, i, p))
    kspec = pl.BlockSpec((1, lk, width), lambda bi, p, i: (bi, 0, p))
    return pl.pallas_call(
        functools.partial(_sb_attn_kernel, tq=tq, tk=tk, q_pos0=q_pos0, n_pair=n_pair, top=top),
        grid=grid,
        in_specs=[qspec, kspec, kspec, qspec,
                  pl.BlockSpec((tk, tk), lambda bi, p, i: (0, 0))],
        out_specs=qspec,
        out_shape=jax.ShapeDtypeStruct((b, sq, D_ATT), BF16),
        scratch_shapes=[pltpu.VMEM((2 * n_pair, tq, LANES), BF16),
                        pltpu.VMEM((2 * n_pair, tq, LANES), F32),
                        pltpu.VMEM((2 * n_pair, tq, LANES), F32),
                        pltpu.SMEM((2,), jnp.int32)],
        compiler_params=pltpu.CompilerParams(
            dimension_semantics=("arbitrary", "arbitrary", "arbitrary"), vmem_limit_bytes=VMEM_LIMIT),
        name="sb_attn",
    )(q, k, v, ga, _tri(tk))


def _tail_kernel(x_ref, ma_ref, cn_ref, gc_ref, wpw_ref, bpw_ref, wout_ref, fnw_ref, y_ref):
    conv = jnp.dot(cn_ref[0], wpw_ref[...], preferred_element_type=F32) + bpw_ref[...]
    mc = (conv * gc_ref[0].astype(F32)).astype(BF16)
    mix = (jnp.dot(ma_ref[0], wout_ref[0:D_ATT], preferred_element_type=F32)
           + jnp.dot(mc, wout_ref[D_ATT:], preferred_element_type=F32))
    xn = x_ref[0] + mix
    r = lax.rsqrt(jnp.mean(xn * xn, axis=-1, keepdims=True) + EPS)
    y_ref[0] = xn * r * fnw_ref[...]


def _tail(x, ma, cn, gc, w_pw_bf, b_pw, w_out_bf, fn_w, ts):
    b, s, _ = x.shape
    assert s % ts == 0
    row = lambda bi, i: (bi, i, 0)
    const = lambda bi, i: (0, 0)
    half = pl.BlockSpec((1, ts, D_ATT), row)
    return pl.pallas_call(
        _tail_kernel,
        grid=(b, s // ts),
        in_specs=[pl.BlockSpec((1, ts, D_MODEL), row), half, half, half,
                  pl.BlockSpec((D_CONV, D_CONV), const),
                  pl.BlockSpec((1, D_CONV), const),
                  pl.BlockSpec((D_MODEL, D_MODEL), const),
                  pl.BlockSpec((1, D_MODEL), const)],
        out_specs=pl.BlockSpec((1, ts, D_MODEL), row),
        out_shape=jax.ShapeDtypeStruct((b, s, D_MODEL), F32),
        compiler_params=pltpu.CompilerParams(
            dimension_semantics=("arbitrary", "arbitrary"), vmem_limit_bytes=VMEM_LIMIT),
        name="tail",
    )(x, ma, cn, gc, w_pw_bf, b_pw, w_out_bf, fn_w)


def _pad_rows(t, n):
    return jnp.pad(t, ((0, 0), (0, n - t.shape[1]), (0, 0)))


def _from_heads_bf(t):
    b, h, s, d = t.shape
    return t.astype(BF16).transpose(0, 2, 1, 3).reshape(b, s, h * d)


def kernel(x_prompt, x_sample, cache_k, cache_v, state_conv, norm_w, w_in, dw_w, dw_b, cn_w, cn_b,
           w_pw, b_pw, w_out, final_norm_w):
    depth = w_in.shape[0]
    assert depth == 1
    l = 0
    b, s, _ = x_prompt.shape
    db, n_new, _ = x_sample.shape
    past = cache_k.shape[3]

    w_in_bf = w_in[l].astype(BF16)
    conv_w = (jnp.broadcast_to(dw_w[l][:, None, :], (CONV_W, SUBLANES, D_CONV)),
              jnp.broadcast_to(dw_b[l][None], (SUBLANES, D_CONV)), cn_w[l][None], cn_b[l][None])
    tail_w = (w_pw[l].astype(BF16), b_pw[l][None], w_out[l].astype(BF16), final_norm_w[None])
    nw = norm_w[l][None]

    tq_p = min(256, s)
    hist0 = jnp.zeros((b, HIST, D_CONV), F32)
    q, k, v, ga, cn, gc, kp, vp, cp = _in_proj(x_prompt, nw, w_in_bf, hist0, *conv_w, ts=min(512, s))
    ma = _sb_attn(q, k, v, ga, tq_p, tq_p, 0, n_pair=4)
    y_p = _tail(x_prompt, ma, cn, gc, *tail_w, ts=min(1024, s))

    tk_s = 256
    hist_s = jnp.pad(state_conv[l], ((0, 0), (HIST_OFF, 0), (0, 0)))
    q, k, v, ga, cn, gc, ks, vs, cs = _in_proj(x_sample, nw, w_in_bf, hist_s, *conv_w, ts=n_new)
    lk = -(-(past + n_new) // tk_s) * tk_s
    k_all = _pad_rows(jnp.concatenate([_from_heads_bf(cache_k[l]), k], axis=1), lk)
    v_all = _pad_rows(jnp.concatenate([_from_heads_bf(cache_v[l]), v], axis=1), lk)
    ma = _sb_attn(q, k_all, v_all, ga, n_new, tk_s, past, n_pair=2)
    y_s = _tail(x_sample, ma, cn, gc, *tail_w, ts=n_new)

    return (y_p, y_s, kp, vp, cp, ks, vs, cs)
```

```python
import functools

import jax
import jax.numpy as jnp
from jax import lax
from jax.experimental import pallas as pl
from jax.experimental.pallas import tpu as pltpu

D_MODEL = 1024
D_ATT = 512
D_CONV = 512
HEAD_DIM = 64
N_HEADS = D_ATT // HEAD_DIM
CONV_W = 31
SUBLANES = 8
LANES = 128
HIST = 32
HIST_OFF = HIST - (CONV_W - 1)
CONV_ROWS = 64
EPS = 1e-6
LN_EPS = 1e-5
D_IN = 4 * D_ATT + 3 * D_CONV
LOG2E = 1.4426950408889634
DEAD_LOG2 = -170.0
VMEM_LIMIT = 56 * 1024 * 1024

BF16 = jnp.bfloat16
F32 = jnp.float32


def _sigmoid(x):
    return 1.0 / (1.0 + jnp.exp(-x))


def _silu(x):
    return x * _sigmoid(x)


def _in_proj_kernel(x_ref, nw_ref, w_ref, hist_ref, dww_ref, dwb_ref, cnw_ref, cnb_ref,
                    q_ref, k_ref, v_ref, ga_ref, cn_ref, gc_ref, ko_ref, vo_ref, cs_ref,
                    ext_ref, sh_ref, *, ts, multi_block):
    i = pl.program_id(1)
    x = x_ref[0]
    r = lax.rsqrt(jnp.mean(x * x, axis=-1, keepdims=True) + EPS)
    h = (x * r * nw_ref[...]).astype(BF16)

    def proj(c):
        return jnp.dot(h, w_ref[:, c * D_ATT:(c + 1) * D_ATT], preferred_element_type=F32)

    @pl.when(i == 0)
    def _():
        ext_ref[0:HIST] = hist_ref[0]

    ext_ref[HIST:HIST + ts] = proj(4) * _sigmoid(proj(5))

    @pl.when(i == pl.num_programs(1) - 1)
    def _():
        cs_ref[0, 0] = ext_ref[ts + HIST_OFF:ts + HIST]

    n_sh = ts + HIST - SUBLANES
    for s in range(1, SUBLANES):
        sh_ref[s - 1] = ext_ref[s:s + n_sh]

    def conv_chunk(c0, rc):
        acc = jnp.broadcast_to(dwb_ref[...][None], (rc // SUBLANES, SUBLANES, D_CONV))
        for w in range(CONV_W):
            s = (HIST_OFF + w) % SUBLANES
            a0 = c0 + HIST_OFF + w - s
            src = ext_ref[a0:a0 + rc] if s == 0 else sh_ref[s - 1, a0:a0 + rc]
            acc = acc + src.reshape(rc // SUBLANES, SUBLANES, D_CONV) * dww_ref[w][None]
        acc = acc.reshape(rc, D_CONV)
        mu = jnp.mean(acc, axis=-1, keepdims=True)
        xc = acc - mu
        var = jnp.mean(xc * xc, axis=-1, keepdims=True)
        cn = _silu(xc * lax.rsqrt(var + LN_EPS) * cnw_ref[...] + cnb_ref[...])
        cn_ref[0, c0:c0 + rc] = cn.astype(BF16)

    rc = min(CONV_ROWS, ts)
    for c0 in range(0, ts, rc):
        conv_chunk(c0, rc)

    q_ref[0] = (proj(0) * (-LOG2E * HEAD_DIM ** -0.5)).astype(BF16)
    for c, bf_ref, heads_ref in ((1, k_ref, ko_ref), (2, v_ref, vo_ref)):
        t = proj(c)
        bf_ref[0] = t.astype(BF16)
        for hd in range(N_HEADS):
            heads_ref[0, 0, hd] = t[:, hd * HEAD_DIM:(hd + 1) * HEAD_DIM]
    ga_ref[0] = _silu(proj(3)).astype(BF16)
    gc_ref[0] = _silu(proj(6)).astype(BF16)

    if multi_block:
        ext_ref[0:HIST] = ext_ref[ts:ts + HIST]


def _in_proj(x, norm_w, w_in_bf, hist, dw_w, dw_b, cn_w, cn_b, ts):
    b, s, _ = x.shape
    assert s % ts == 0 and (ts >= HIST or s == ts)
    grid = (b, s // ts)
    row = lambda bi, i: (bi, i, 0)
    const = lambda bi, i: (0, 0)
    heads = lambda bi, i: (0, bi, 0, i, 0)
    bf_spec = pl.BlockSpec((1, ts, D_ATT), row)
    vec = pl.BlockSpec((1, D_CONV), const)
    out_shape = [
        jax.ShapeDtypeStruct((b, s, D_ATT), BF16),
        jax.ShapeDtypeStruct((b, s, D_ATT), BF16),
        jax.ShapeDtypeStruct((b, s, D_ATT), BF16),
        jax.ShapeDtypeStruct((b, s, D_ATT), BF16),
        jax.ShapeDtypeStruct((b, s, D_CONV), BF16),
        jax.ShapeDtypeStruct((b, s, D_CONV), BF16),
        jax.ShapeDtypeStruct((1, b, N_HEADS, s, HEAD_DIM), F32),
        jax.ShapeDtypeStruct((1, b, N_HEADS, s, HEAD_DIM), F32),
        jax.ShapeDtypeStruct((1, b, CONV_W - 1, D_CONV), F32),
    ]
    out_specs = [bf_spec] * 6 + [
        pl.BlockSpec((1, 1, N_HEADS, ts, HEAD_DIM), heads),
        pl.BlockSpec((1, 1, N_HEADS, ts, HEAD_DIM), heads),
        pl.BlockSpec((1, 1, CONV_W - 1, D_CONV), lambda bi, i: (0, bi, 0, 0))]
    return pl.pallas_call(
        functools.partial(_in_proj_kernel, ts=ts, multi_block=s > ts),
        grid=grid,
        in_specs=[pl.BlockSpec((1, ts, D_MODEL), row),
                  pl.BlockSpec((1, D_MODEL), const),
                  pl.BlockSpec((D_MODEL, D_IN), const),
                  pl.BlockSpec((1, HIST, D_CONV), lambda bi, i: (bi, 0, 0)),
                  pl.BlockSpec((CONV_W, SUBLANES, D_CONV), lambda bi, i: (0, 0, 0)),
                  pl.BlockSpec((SUBLANES, D_CONV), const),
                  vec, vec],
        out_specs=out_specs,
        out_shape=out_shape,
        scratch_shapes=[pltpu.VMEM((HIST + ts, D_CONV), F32),
                        pltpu.VMEM((SUBLANES - 1, HIST + ts - SUBLANES, D_CONV), F32)],
        compiler_params=pltpu.CompilerParams(
            dimension_semantics=("arbitrary", "arbitrary"), vmem_limit_bytes=VMEM_LIMIT),
        name="in_proj",
    )(x, norm_w, w_in_bf, hist, dw_w, dw_b, cn_w, cn_b)


def _sb_attn_kernel(q_ref, k_ref, v_ref, ga_ref, tri_ref, x_ref, cn_ref, gc_ref, wpw_ref, bpw_ref, wout_ref,
                    fnw_ref, y_ref, qm_ref, acc_ref, carry_ref, live_ref, *, tq, tk, q_pos0, top):
    i = pl.program_id(1)
    n_pair = D_ATT // LANES
    n_head = 2 * n_pair
    q_start = q_pos0 + i * tq
    j_diag = q_start // tk
    lane = lax.broadcasted_iota(jnp.int32, (tq, LANES), 1)
    zero = jnp.zeros((tq, LANES), BF16)
    for p in range(n_pair):
        q2 = q_ref[0, :, p * LANES:(p + 1) * LANES]
        qm_ref[2 * p] = jnp.where(lane < HEAD_DIM, q2, zero)
        qm_ref[2 * p + 1] = jnp.where(lane >= HEAD_DIM, q2, zero)
    acc_ref[...] = jnp.zeros_like(acc_ref)
    carry_ref[...] = jnp.zeros_like(carry_ref)
    heads = range(n_head)
    cols = [slice((hh // 2) * LANES, (hh // 2 + 1) * LANES) for hh in heads]

    def tile(j, masked, r0, r1, kw, merge=False):
        rows = r1 - r0
        k0 = pl.multiple_of(j * tk, tk)
        kj = k_ref[0, pl.ds(k0, kw), :]
        vj = v_ref[0, pl.ds(k0, kw), :]
        if masked:
            q_pos = q_start + r0 + lax.broadcasted_iota(jnp.int32, (rows, kw), 0)
            k_pos = k0 + lax.broadcasted_iota(jnp.int32, (rows, kw), 1)
            valid = k_pos < q_pos
        ws = [lax.dot_general(qm_ref[hh, r0:r1], kj[:, cols[hh]], (((1,), (1,)), ((), ())),
                              preferred_element_type=F32) for hh in heads]
        lfs = []
        for hh in heads:
            w = ws[hh]
            lf = jnp.minimum(w, 0.0) - jnp.log2(1.0 + jnp.exp2(jnp.minimum(w, -w)))
            if masked:
                lf = jnp.where(valid, lf, 0.0)
            lfs.append(lf.astype(BF16))
        css = [jnp.dot(lfs[hh], tri_ref[0:kw, 0:kw], preferred_element_type=F32) for hh in heads]
        ats = []
        worst = None
        for hh in heads:
            t1 = css[hh] + jnp.concatenate([carry_ref[hh, r0:r1]] * (kw // LANES), axis=1)
            a = jnp.exp2(t1 - ws[hh])
            if masked:
                a = jnp.where(valid, a, 0.0)
            ats.append(a.astype(BF16))
            total = t1[:, 0:1]
            carry_ref[hh, r0:r1] = jnp.broadcast_to(total, (rows, LANES))
            worst = total if worst is None else jnp.maximum(worst, total)
        for hh in heads:
            acc_ref[hh, r0:r1] += jnp.dot(ats[hh], vj[:, cols[hh]], preferred_element_type=F32)
        alive = lambda lo, hi: (jnp.max(worst[lo - r0:hi - r0]) >= DEAD_LOG2).astype(jnp.int32)
        if r0 < top:
            low = alive(r0, min(r1, top))
            live_ref[0] = jnp.maximum(live_ref[0], low) if merge else low
        if not merge:
            live_ref[1] = jnp.int32(0)
        if r1 > top:
            live_ref[1] = alive(max(r0, top), r1)

    half = tq // 2
    if tq == tk and half % LANES == 0 and half <= top:
        tile(j_diag, True, 0, half, half)
        tile(j_diag, True, half, tq, tk, merge=True)
    else:
        tile(j_diag, True, 0, tq, tk)

    def body(c):
        j, _, live_bottom = c
        if top < tq:
            @pl.when(live_bottom > 0)
            def _():
                tile(j, False, 0, tq, tk)

            @pl.when(live_bottom == 0)
            def _():
                tile(j, False, 0, top, tk)
        else:
            tile(j, False, 0, tq, tk)
        return j - 1, live_ref[0], live_ref[1]

    lax.while_loop(lambda c: jnp.logical_and(c[0] >= 0, c[1] + c[2] > 0), body,
                   (j_diag - 1, live_ref[0], live_ref[1]))
    gated = []
    for p in range(n_pair):
        att = jnp.where(lane < HEAD_DIM, acc_ref[2 * p], acc_ref[2 * p + 1])
        gated.append((att * ga_ref[0, :, p * LANES:(p + 1) * LANES].astype(F32)).astype(BF16))
    ma = jnp.concatenate(gated, axis=1)
    conv = jnp.dot(cn_ref[0], wpw_ref[...], preferred_element_type=F32) + bpw_ref[...]
    mc = (conv * gc_ref[0].astype(F32)).astype(BF16)
    mix = (jnp.dot(ma, wout_ref[0:D_ATT], preferred_element_type=F32)
           + jnp.dot(mc, wout_ref[D_ATT:], preferred_element_type=F32))
    xn = x_ref[0] + mix
    r = lax.rsqrt(jnp.mean(xn * xn, axis=-1, keepdims=True) + EPS)
    y_ref[0] = xn * r * fnw_ref[...]


def _tri(tk):
    j = jnp.arange(tk)[:, None]
    s = jnp.arange(tk)[None, :]
    return (j >= s).astype(BF16)


def _sb_attn_tail(q, k, v, ga, x, cn, gc, w_pw_bf, b_pw, w_out_bf, fn_w, tq, tk, q_pos0):
    b, sq, _ = q.shape
    lk = k.shape[1]
    assert tk % tq == 0 and q_pos0 % tq == 0 and lk % tk == 0 and q_pos0 + sq <= lk
    top = 3 * tq // 4 if tq % 64 == 0 else tq
    row = lambda bi, i: (bi, i, 0)
    const = lambda bi, i: (0, 0)
    half = pl.BlockSpec((1, tq, D_ATT), row)
    full = pl.BlockSpec((1, tq, D_MODEL), row)
    kspec = pl.BlockSpec((1, lk, D_ATT), lambda bi, i: (bi, 0, 0))
    return pl.pallas_call(
        functools.partial(_sb_attn_kernel, tq=tq, tk=tk, q_pos0=q_pos0, top=top),
        grid=(b, sq // tq),
        in_specs=[half, kspec, kspec, half,
                  pl.BlockSpec((tk, tk), const),
                  full, half, half,
                  pl.BlockSpec((D_CONV, D_CONV), const),
                  pl.BlockSpec((1, D_CONV), const),
                  pl.BlockSpec((D_MODEL, D_MODEL), const),
                  pl.BlockSpec((1, D_MODEL), const)],
        out_specs=full,
        out_shape=jax.ShapeDtypeStruct((b, sq, D_MODEL), F32),
        scratch_shapes=[pltpu.VMEM((N_HEADS, tq, LANES), BF16),
                        pltpu.VMEM((N_HEADS, tq, LANES), F32),
                        pltpu.VMEM((N_HEADS, tq, LANES), F32),
                        pltpu.SMEM((2,), jnp.int32)],
        compiler_params=pltpu.CompilerParams(
            dimension_semantics=("arbitrary", "arbitrary"), vmem_limit_bytes=VMEM_LIMIT),
        name="sb_attn_tail",
    )(q, k, v, ga, _tri(tk), x, cn, gc, w_pw_bf, b_pw, w_out_bf, fn_w)


def _pad_rows(t, n):
    return jnp.pad(t, ((0, 0), (0, n - t.shape[1]), (0, 0)))


def _from_heads_bf(t):
    b, h, s, d = t.shape
    return t.astype(BF16).transpose(0, 2, 1, 3).reshape(b, s, h * d)


def kernel(x_prompt, x_sample, cache_k, cache_v, state_conv, norm_w, w_in, dw_w, dw_b, cn_w, cn_b,
           w_pw, b_pw, w_out, final_norm_w):
    depth = w_in.shape[0]
    assert depth == 1
    l = 0
    b, s, _ = x_prompt.shape
    db, n_new, _ = x_sample.shape
    past = cache_k.shape[3]

    w_in_bf = w_in[l].astype(BF16)
    conv_w = (jnp.broadcast_to(dw_w[l][:, None, :], (CONV_W, SUBLANES, D_CONV)),
              jnp.broadcast_to(dw_b[l][None], (SUBLANES, D_CONV)), cn_w[l][None], cn_b[l][None])
    tail_w = (w_pw[l].astype(BF16), b_pw[l][None], w_out[l].astype(BF16), final_norm_w[None])
    nw = norm_w[l][None]

    tq_p = min(256, s)
    hist0 = jnp.zeros((b, HIST, D_CONV), F32)
    q, k, v, ga, cn, gc, kp, vp, cp = _in_proj(x_prompt, nw, w_in_bf, hist0, *conv_w, ts=min(512, s))
    y_p = _sb_attn_tail(q, k, v, ga, x_prompt, cn, gc, *tail_w, tq=tq_p, tk=tq_p, q_pos0=0)

    tk_s = 256
    hist_s = jnp.pad(state_conv[l], ((0, 0), (HIST_OFF, 0), (0, 0)))
    q, k, v, ga, cn, gc, ks, vs, cs = _in_proj(x_sample, nw, w_in_bf, hist_s, *conv_w, ts=n_new)
    lk = -(-(past + n_new) // tk_s) * tk_s
    k_all = _pad_rows(jnp.concatenate([_from_heads_bf(cache_k[l]), k], axis=1), lk)
    v_all = _pad_rows(jnp.concatenate([_from_heads_bf(cache_v[l]), v], axis=1), lk)
    y_s = _sb_attn_tail(q, k_all, v_all, ga, x_sample, cn, gc, *tail_w, tq=n_new, tk=tk_s, q_pos0=past)

    return (y_p, y_s, kp, vp, cp, ks, vs, cs)
```

```python
import functools

import jax
import jax.numpy as jnp
from jax import lax
from jax.experimental import pallas as pl
from jax.experimental.pallas import tpu as pltpu

D_MODEL = 1024
D_ATT = 512
D_CONV = 512
HEAD_DIM = 64
N_HEADS = D_ATT // HEAD_DIM
CONV_W = 31
SUBLANES = 8
LANES = 128
HIST = 32
HIST_OFF = HIST - (CONV_W - 1)
CONV_ROWS = 64
EPS = 1e-6
LN_EPS = 1e-5
D_IN = 4 * D_ATT + 3 * D_CONV
LOG2E = 1.4426950408889634
DEAD_LOG2 = -170.0
VMEM_LIMIT = 56 * 1024 * 1024

BF16 = jnp.bfloat16
F32 = jnp.float32


def _sigmoid(x):
    return 1.0 / (1.0 + jnp.exp(-x))


def _silu(x):
    return x * _sigmoid(x)


def _in_proj_kernel(x_ref, nw_ref, w_ref, hist_ref, dww_ref, dwb_ref, cnw_ref, cnb_ref,
                    q_ref, k_ref, v_ref, ga_ref, cn_ref, gc_ref, ko_ref, vo_ref, cs_ref,
                    ext_ref, seg_ref, conv_ref, *, ts, pitch, multi_block):
    i = pl.program_id(1)
    x = x_ref[0]
    r = lax.rsqrt(jnp.mean(x * x, axis=-1, keepdims=True) + EPS)
    h = (x * r * nw_ref[...]).astype(BF16)

    def proj(c):
        return jnp.dot(h, w_ref[:, c * D_ATT:(c + 1) * D_ATT], preferred_element_type=F32)

    @pl.when(i == 0)
    def _():
        ext_ref[0:HIST] = hist_ref[0]

    ext_ref[HIST:HIST + ts] = proj(4) * _sigmoid(proj(5))

    @pl.when(i == pl.num_programs(1) - 1)
    def _():
        cs_ref[0, 0] = ext_ref[ts + HIST_OFF:ts + HIST]

    seg = ts // SUBLANES
    n_slab = D_CONV // LANES
    for r8 in range(SUBLANES):
        for c in range(n_slab):
            seg_ref[c, r8 * pitch:r8 * pitch + HIST + seg] = (
                ext_ref[r8 * seg:r8 * seg + HIST + seg, c * LANES:(c + 1) * LANES])

    group = min(SUBLANES, seg)

    def tap_group(g, carry):
        a0 = g * group
        for c in range(n_slab):
            cs = slice(c * LANES, (c + 1) * LANES)
            accs = [dwb_ref[:, cs]] * group
            for w in range(CONV_W):
                tap = dww_ref[w, :, cs]
                accs = [acc + seg_ref[c, pl.ds(a0 + ai + HIST_OFF + w, SUBLANES, stride=pitch), :] * tap
                        for ai, acc in enumerate(accs)]
            for ai, acc in enumerate(accs):
                conv_ref[c, pl.ds(a0 + ai, SUBLANES, stride=seg), :] = acc
        return carry

    lax.fori_loop(0, seg // group, tap_group, 0)

    def conv_chunk(c0, rc):
        acc = jnp.concatenate([conv_ref[c, c0:c0 + rc] for c in range(n_slab)], axis=1)
        mu = jnp.mean(acc, axis=-1, keepdims=True)
        xc = acc - mu
        var = jnp.mean(xc * xc, axis=-1, keepdims=True)
        cn = _silu(xc * lax.rsqrt(var + LN_EPS) * cnw_ref[...] + cnb_ref[...])
        cn_ref[0, c0:c0 + rc] = cn.astype(BF16)

    rc = min(CONV_ROWS, ts)
    for c0 in range(0, ts, rc):
        conv_chunk(c0, rc)

    q_ref[0] = (proj(0) * (-LOG2E * HEAD_DIM ** -0.5)).astype(BF16)
    for c, bf_ref, heads_ref in ((1, k_ref, ko_ref), (2, v_ref, vo_ref)):
        t = proj(c)
        bf_ref[0] = t.astype(BF16)
        for hd in range(N_HEADS):
            heads_ref[0, 0, hd] = t[:, hd * HEAD_DIM:(hd + 1) * HEAD_DIM]
    ga_ref[0] = _silu(proj(3)).astype(BF16)
    gc_ref[0] = _silu(proj(6)).astype(BF16)

    if multi_block:
        ext_ref[0:HIST] = ext_ref[ts:ts + HIST]


def _in_proj(x, norm_w, w_in_bf, hist, dw_w, dw_b, cn_w, cn_b, ts):
    b, s, _ = x.shape
    assert s % ts == 0 and ts % SUBLANES == 0 and (ts >= HIST or s == ts)
    pitch = -(-(HIST + ts // SUBLANES) // 4)
    pitch = 4 * (pitch + 1 - pitch % 2)
    grid = (b, s // ts)
    row = lambda bi, i: (bi, i, 0)
    const = lambda bi, i: (0, 0)
    heads = lambda bi, i: (0, bi, 0, i, 0)
    bf_spec = pl.BlockSpec((1, ts, D_ATT), row)
    vec = pl.BlockSpec((1, D_CONV), const)
    out_shape = [
        jax.ShapeDtypeStruct((b, s, D_ATT), BF16),
        jax.ShapeDtypeStruct((b, s, D_ATT), BF16),
        jax.ShapeDtypeStruct((b, s, D_ATT), BF16),
        jax.ShapeDtypeStruct((b, s, D_ATT), BF16),
        jax.ShapeDtypeStruct((b, s, D_CONV), BF16),
        jax.ShapeDtypeStruct((b, s, D_CONV), BF16),
        jax.ShapeDtypeStruct((1, b, N_HEADS, s, HEAD_DIM), F32),
        jax.ShapeDtypeStruct((1, b, N_HEADS, s, HEAD_DIM), F32),
        jax.ShapeDtypeStruct((1, b, CONV_W - 1, D_CONV), F32),
    ]
    out_specs = [bf_spec] * 6 + [
        pl.BlockSpec((1, 1, N_HEADS, ts, HEAD_DIM), heads),
        pl.BlockSpec((1, 1, N_HEADS, ts, HEAD_DIM), heads),
        pl.BlockSpec((1, 1, CONV_W - 1, D_CONV), lambda bi, i: (0, bi, 0, 0))]
    return pl.pallas_call(
        functools.partial(_in_proj_kernel, ts=ts, pitch=pitch, multi_block=s > ts),
        grid=grid,
        in_specs=[pl.BlockSpec((1, ts, D_MODEL), row),
                  pl.BlockSpec((1, D_MODEL), const),
                  pl.BlockSpec((D_MODEL, D_IN), const),
                  pl.BlockSpec((1, HIST, D_CONV), lambda bi, i: (bi, 0, 0)),
                  pl.BlockSpec((CONV_W, SUBLANES, D_CONV), lambda bi, i: (0, 0, 0)),
                  pl.BlockSpec((SUBLANES, D_CONV), const),
                  vec, vec],
        out_specs=out_specs,
        out_shape=out_shape,
        scratch_shapes=[pltpu.VMEM((HIST + ts, D_CONV), F32),
                        pltpu.VMEM((D_CONV // LANES, SUBLANES * pitch, LANES), F32),
                        pltpu.VMEM((D_CONV // LANES, ts, LANES), F32)],
        compiler_params=pltpu.CompilerParams(
            dimension_semantics=("arbitrary", "arbitrary"), vmem_limit_bytes=VMEM_LIMIT),
        name="in_proj",
    )(x, norm_w, w_in_bf, hist, dw_w, dw_b, cn_w, cn_b)


def _sb_attn_kernel(q_ref, k_ref, v_ref, ga_ref, tri_ref, x_ref, cn_ref, gc_ref, wpw_ref, bpw_ref, wout_ref,
                    fnw_ref, y_ref, qm_ref, acc_ref, carry_ref, live_ref, *, tq, tk, q_pos0, top):
    i = pl.program_id(1)
    n_pair = D_ATT // LANES
    n_head = 2 * n_pair
    q_start = q_pos0 + i * tq
    j_diag = q_start // tk
    lane = lax.broadcasted_iota(jnp.int32, (tq, LANES), 1)
    zero = jnp.zeros((tq, LANES), BF16)
    for p in range(n_pair):
        q2 = q_ref[0, :, p * LANES:(p + 1) * LANES]
        qm_ref[2 * p] = jnp.where(lane < HEAD_DIM, q2, zero)
        qm_ref[2 * p + 1] = jnp.where(lane >= HEAD_DIM, q2, zero)
    acc_ref[...] = jnp.zeros_like(acc_ref)
    carry_ref[...] = jnp.zeros_like(carry_ref)
    heads = range(n_head)
    cols = [slice((hh // 2) * LANES, (hh // 2 + 1) * LANES) for hh in heads]

    def tile(j, masked, r0, r1, kw, merge=False):
        rows = r1 - r0
        k0 = pl.multiple_of(j * tk, tk)
        kj = k_ref[0, pl.ds(k0, kw), :]
        vj = v_ref[0, pl.ds(k0, kw), :]
        if masked:
            q_pos = q_start + r0 + lax.broadcasted_iota(jnp.int32, (rows, kw), 0)
            k_pos = k0 + lax.broadcasted_iota(jnp.int32, (rows, kw), 1)
            valid = k_pos < q_pos
        ws = [lax.dot_general(qm_ref[hh, r0:r1], kj[:, cols[hh]], (((1,), (1,)), ((), ())),
                              preferred_element_type=F32) for hh in heads]
        lfs = []
        for hh in heads:
            w = ws[hh]
            lf = jnp.minimum(w, 0.0) - jnp.log2(1.0 + jnp.exp2(jnp.minimum(w, -w)))
            if masked:
                lf = jnp.where(valid, lf, 0.0)
            lfs.append(lf.astype(BF16))
        css = [jnp.dot(lfs[hh], tri_ref[0:kw, 0:kw], preferred_element_type=F32) for hh in heads]
        ats = []
        worst = None
        for hh in heads:
            t1 = css[hh] + jnp.concatenate([carry_ref[hh, r0:r1]] * (kw // LANES), axis=1)
            a = jnp.exp2(t1 - ws[hh])
            if masked:
                a = jnp.where(valid, a, 0.0)
            ats.append(a.astype(BF16))
            total = t1[:, 0:1]
            carry_ref[hh, r0:r1] = jnp.broadcast_to(total, (rows, LANES))
            worst = total if worst is None else jnp.maximum(worst, total)
        for hh in heads:
            acc_ref[hh, r0:r1] += jnp.dot(ats[hh], vj[:, cols[hh]], preferred_element_type=F32)
        alive = lambda lo, hi: (jnp.max(worst[lo - r0:hi - r0]) >= DEAD_LOG2).astype(jnp.int32)
        if r0 < top:
            low = alive(r0, min(r1, top))
            live_ref[0] = jnp.maximum(live_ref[0], low) if merge else low
        if not merge:
            live_ref[1] = jnp.int32(0)
        if r1 > top:
            live_ref[1] = alive(max(r0, top), r1)

    half = tq // 2
    if tq == tk and half % LANES == 0 and half <= top:
        tile(j_diag, True, 0, half, half)
        tile(j_diag, True, half, tq, tk, merge=True)
    else:
        tile(j_diag, True, 0, tq, tk)

    def body(c):
        j, _, live_bottom = c
        if top < tq:
            @pl.when(live_bottom > 0)
            def _():
                tile(j, False, 0, tq, tk)

            @pl.when(live_bottom == 0)
            def _():
                tile(j, False, 0, top, tk)
        else:
            tile(j, False, 0, tq, tk)
        return j - 1, live_ref[0], live_ref[1]

    lax.while_loop(lambda c: jnp.logical_and(c[0] >= 0, c[1] + c[2] > 0), body,
                   (j_diag - 1, live_ref[0], live_ref[1]))
    gated = []
    for p in range(n_pair):
        att = jnp.where(lane < HEAD_DIM, acc_ref[2 * p], acc_ref[2 * p + 1])
        gated.append((att * ga_ref[0, :, p * LANES:(p + 1) * LANES].astype(F32)).astype(BF16))
    ma = jnp.concatenate(gated, axis=1)
    conv = jnp.dot(cn_ref[0], wpw_ref[...], preferred_element_type=F32) + bpw_ref[...]
    mc = (conv * gc_ref[0].astype(F32)).astype(BF16)
    mix = (jnp.dot(ma, wout_ref[0:D_ATT], preferred_element_type=F32)
           + jnp.dot(mc, wout_ref[D_ATT:], preferred_element_type=F32))
    xn = x_ref[0] + mix
    r = lax.rsqrt(jnp.mean(xn * xn, axis=-1, keepdims=True) + EPS)
    y_ref[0] = xn * r * fnw_ref[...]


def _tri(tk):
    j = jnp.arange(tk)[:, None]
    s = jnp.arange(tk)[None, :]
    return (j >= s).astype(BF16)


def _sb_attn_tail(q, k, v, ga, x, cn, gc, w_pw_bf, b_pw, w_out_bf, fn_w, tq, tk, q_pos0):
    b, sq, _ = q.shape
    lk = k.shape[1]
    assert tk % tq == 0 and q_pos0 % tq == 0 and lk % tk == 0 and q_pos0 + sq <= lk
    top = 3 * tq // 4 if tq % 64 == 0 else tq
    row = lambda bi, i: (bi, i, 0)
    const = lambda bi, i: (0, 0)
    half = pl.BlockSpec((1, tq, D_ATT), row)
    full = pl.BlockSpec((1, tq, D_MODEL), row)
    kspec = pl.BlockSpec((1, lk, D_ATT), lambda bi, i: (bi, 0, 0))
    return pl.pallas_call(
        functools.partial(_sb_attn_kernel, tq=tq, tk=tk, q_pos0=q_pos0, top=top),
        grid=(b, sq // tq),
        in_specs=[half, kspec, kspec, half,
                  pl.BlockSpec((tk, tk), const),
                  full, half, half,
                  pl.BlockSpec((D_CONV, D_CONV), const),
                  pl.BlockSpec((1, D_CONV), const),
                  pl.BlockSpec((D_MODEL, D_MODEL), const),
                  pl.BlockSpec((1, D_MODEL), const)],
        out_specs=full,
        out_shape=jax.ShapeDtypeStruct((b, sq, D_MODEL), F32),
        scratch_shapes=[pltpu.VMEM((N_HEADS, tq, LANES), BF16),
                        pltpu.VMEM((N_HEADS, tq, LANES), F32),
                        pltpu.VMEM((N_HEADS, tq, LANES), F32),
                        pltpu.SMEM((2,), jnp.int32)],
        compiler_params=pltpu.CompilerParams(
            dimension_semantics=("arbitrary", "arbitrary"), vmem_limit_bytes=VMEM_LIMIT),
        name="sb_attn_tail",
    )(q, k, v, ga, _tri(tk), x, cn, gc, w_pw_bf, b_pw, w_out_bf, fn_w)


def _pad_rows(t, n):
    return jnp.pad(t, ((0, 0), (0, n - t.shape[1]), (0, 0)))


def _from_heads_bf(t):
    b, h, s, d = t.shape
    return t.astype(BF16).transpose(0, 2, 1, 3).reshape(b, s, h * d)


def kernel(x_prompt, x_sample, cache_k, cache_v, state_conv, norm_w, w_in, dw_w, dw_b, cn_w, cn_b,
           w_pw, b_pw, w_out, final_norm_w):
    depth = w_in.shape[0]
    assert depth == 1
    l = 0
    b, s, _ = x_prompt.shape
    db, n_new, _ = x_sample.shape
    past = cache_k.shape[3]

    w_in_bf = w_in[l].astype(BF16)
    conv_w = (jnp.broadcast_to(dw_w[l][:, None, :], (CONV_W, SUBLANES, D_CONV)),
              jnp.broadcast_to(dw_b[l][None], (SUBLANES, D_CONV)), cn_w[l][None], cn_b[l][None])
    tail_w = (w_pw[l].astype(BF16), b_pw[l][None], w_out[l].astype(BF16), final_norm_w[None])
    nw = norm_w[l][None]

    tq_p = min(256, s)
    hist0 = jnp.zeros((b, HIST, D_CONV), F32)
    q, k, v, ga, cn, gc, kp, vp, cp = _in_proj(x_prompt, nw, w_in_bf, hist0, *conv_w, ts=min(512, s))
    y_p = _sb_attn_tail(q, k, v, ga, x_prompt, cn, gc, *tail_w, tq=tq_p, tk=tq_p, q_pos0=0)

    tk_s = 256
    hist_s = jnp.pad(state_conv[l], ((0, 0), (HIST_OFF, 0), (0, 0)))
    q, k, v, ga, cn, gc, ks, vs, cs = _in_proj(x_sample, nw, w_in_bf, hist_s, *conv_w, ts=n_new)
    lk = -(-(past + n_new) // tk_s) * tk_s
    k_all = _pad_rows(jnp.concatenate([_from_heads_bf(cache_k[l]), k], axis=1), lk)
    v_all = _pad_rows(jnp.concatenate([_from_heads_bf(cache_v[l]), v], axis=1), lk)
    y_s = _sb_attn_tail(q, k_all, v_all, ga, x_sample, cn, gc, *tail_w, tq=n_new, tk=tk_s, q_pos0=past)

    return (y_p, y_s, kp, vp, cp, ks, vs, cs)
```

```python
import functools

import jax
import jax.numpy as jnp
from jax import lax
from jax.experimental import pallas as pl
from jax.experimental.pallas import tpu as pltpu

D_MODEL = 1024
D_ATT = 512
D_CONV = 512
HEAD_DIM = 64
N_HEADS = D_ATT // HEAD_DIM
CONV_W = 31
SUBLANES = 8
LANES = 128
HIST = 32
HIST_OFF = HIST - (CONV_W - 1)
CONV_ROWS = 64
EPS = 1e-6
LN_EPS = 1e-5
D_IN = 4 * D_ATT + 3 * D_CONV
LOG2E = 1.4426950408889634
DEAD_LOG2 = -170.0
VMEM_LIMIT = 56 * 1024 * 1024

BF16 = jnp.bfloat16
F32 = jnp.float32


def _sigmoid(x):
    return 1.0 / (1.0 + jnp.exp(-x))


def _silu(x):
    return x * _sigmoid(x)


def _in_proj_kernel(x_ref, nw_ref, w_ref, hist_ref, dww_ref, dwb_ref, cnw_ref, cnb_ref,
                    q_ref, k_ref, v_ref, ga_ref, cn_ref, gc_ref, ko_ref, vo_ref, cs_ref,
                    ext_ref, seg_ref, conv_ref, *, ts, pitch, multi_block):
    i = pl.program_id(1)
    x = x_ref[0]
    r = lax.rsqrt(jnp.mean(x * x, axis=-1, keepdims=True) + EPS)
    h = (x * r * nw_ref[...]).astype(BF16)

    def proj(c):
        return jnp.dot(h, w_ref[:, c * D_ATT:(c + 1) * D_ATT], preferred_element_type=F32)

    @pl.when(i == 0)
    def _():
        ext_ref[0:HIST] = hist_ref[0]

    ext_ref[HIST:HIST + ts] = proj(4) * _sigmoid(proj(5))

    @pl.when(i == pl.num_programs(1) - 1)
    def _():
        cs_ref[0, 0] = ext_ref[ts + HIST_OFF:ts + HIST]

    seg = ts // SUBLANES
    n_slab = D_CONV // LANES
    for r8 in range(SUBLANES):
        for c in range(n_slab):
            seg_ref[c, r8 * pitch:r8 * pitch + HIST + seg] = (
                ext_ref[r8 * seg:r8 * seg + HIST + seg, c * LANES:(c + 1) * LANES])

    group = min(SUBLANES, seg)

    def tap_group(g, carry):
        a0 = g * group
        for c in range(n_slab):
            cs = slice(c * LANES, (c + 1) * LANES)
            accs = [dwb_ref[:, cs]] * group
            for w in range(CONV_W):
                tap = dww_ref[w, :, cs]
                accs = [acc + seg_ref[c, pl.ds(a0 + ai + HIST_OFF + w, SUBLANES, stride=pitch), :] * tap
                        for ai, acc in enumerate(accs)]
            for ai, acc in enumerate(accs):
                conv_ref[c, pl.ds(a0 + ai, SUBLANES, stride=seg), :] = acc
        return carry

    lax.fori_loop(0, seg // group, tap_group, 0)

    def conv_chunk(c0, rc):
        acc = jnp.concatenate([conv_ref[c, c0:c0 + rc] for c in range(n_slab)], axis=1)
        mu = jnp.mean(acc, axis=-1, keepdims=True)
        xc = acc - mu
        var = jnp.mean(xc * xc, axis=-1, keepdims=True)
        cn = _silu(xc * lax.rsqrt(var + LN_EPS) * cnw_ref[...] + cnb_ref[...])
        cn_ref[0, c0:c0 + rc] = cn.astype(BF16)

    rc = min(CONV_ROWS, ts)
    for c0 in range(0, ts, rc):
        conv_chunk(c0, rc)

    q_ref[0] = (proj(0) * (-LOG2E * HEAD_DIM ** -0.5)).astype(BF16)
    for c, bf_ref, heads_ref in ((1, k_ref, ko_ref), (2, v_ref, vo_ref)):
        t = proj(c)
        bf_ref[0] = t.astype(BF16)
        for hd in range(N_HEADS):
            heads_ref[0, 0, hd] = t[:, hd * HEAD_DIM:(hd + 1) * HEAD_DIM]
    ga_ref[0] = _silu(proj(3)).astype(BF16)
    gc_ref[0] = _silu(proj(6)).astype(BF16)

    if multi_block:
        ext_ref[0:HIST] = ext_ref[ts:ts + HIST]


def _in_proj(x, norm_w, w_in_bf, hist, dw_w, dw_b, cn_w, cn_b, ts):
    b, s, _ = x.shape
    assert s % ts == 0 and ts % SUBLANES == 0 and (ts >= HIST or s == ts)
    pitch = -(-(HIST + ts // SUBLANES) // 4)
    pitch = 4 * (pitch + 1 - pitch % 2)
    grid = (b, s // ts)
    row = lambda bi, i: (bi, i, 0)
    const = lambda bi, i: (0, 0)
    heads = lambda bi, i: (0, bi, 0, i, 0)
    bf_spec = pl.BlockSpec((1, ts, D_ATT), row)
    vec = pl.BlockSpec((1, D_CONV), const)
    out_shape = [
        jax.ShapeDtypeStruct((b, s, D_ATT), BF16),
        jax.ShapeDtypeStruct((b, s, D_ATT), BF16),
        jax.ShapeDtypeStruct((b, s, D_ATT), BF16),
        jax.ShapeDtypeStruct((b, s, D_ATT), BF16),
        jax.ShapeDtypeStruct((b, s, D_CONV), BF16),
        jax.ShapeDtypeStruct((b, s, D_CONV), BF16),
        jax.ShapeDtypeStruct((1, b, N_HEADS, s, HEAD_DIM), F32),
        jax.ShapeDtypeStruct((1, b, N_HEADS, s, HEAD_DIM), F32),
        jax.ShapeDtypeStruct((1, b, CONV_W - 1, D_CONV), F32),
    ]
    out_specs = [bf_spec] * 6 + [
        pl.BlockSpec((1, 1, N_HEADS, ts, HEAD_DIM), heads),
        pl.BlockSpec((1, 1, N_HEADS, ts, HEAD_DIM), heads),
        pl.BlockSpec((1, 1, CONV_W - 1, D_CONV), lambda bi, i: (0, bi, 0, 0))]
    return pl.pallas_call(
        functools.partial(_in_proj_kernel, ts=ts, pitch=pitch, multi_block=s > ts),
        grid=grid,
        in_specs=[pl.BlockSpec((1, ts, D_MODEL), row),
                  pl.BlockSpec((1, D_MODEL), const),
                  pl.BlockSpec((D_MODEL, D_IN), const),
                  pl.BlockSpec((1, HIST, D_CONV), lambda bi, i: (bi, 0, 0)),
                  pl.BlockSpec((CONV_W, SUBLANES, D_CONV), lambda bi, i: (0, 0, 0)),
                  pl.BlockSpec((SUBLANES, D_CONV), const),
                  vec, vec],
        out_specs=out_specs,
        out_shape=out_shape,
        scratch_shapes=[pltpu.VMEM((HIST + ts, D_CONV), F32),
                        pltpu.VMEM((D_CONV // LANES, SUBLANES * pitch, LANES), F32),
                        pltpu.VMEM((D_CONV // LANES, ts, LANES), F32)],
        compiler_params=pltpu.CompilerParams(
            dimension_semantics=("arbitrary", "arbitrary"), vmem_limit_bytes=VMEM_LIMIT),
        name="in_proj",
    )(x, norm_w, w_in_bf, hist, dw_w, dw_b, cn_w, cn_b)


def _sb_attn_kernel(q_ref, k_ref, v_ref, ga_ref, tri_ref, x_ref, cn_ref, gc_ref, wpw_ref, bpw_ref, wout_ref,
                    fnw_ref, y_ref, qm_ref, acc_ref, carry_ref, live_ref, *, tq, tk, q_pos0, top):
    i = pl.program_id(1)
    n_pair = D_ATT // LANES
    n_head = 2 * n_pair
    q_start = q_pos0 + i * tq
    j_diag = q_start // tk
    lane = lax.broadcasted_iota(jnp.int32, (tq, LANES), 1)
    zero = jnp.zeros((tq, LANES), BF16)
    for p in range(n_pair):
        q2 = q_ref[0, :, p * LANES:(p + 1) * LANES]
        qm_ref[2 * p] = jnp.where(lane < HEAD_DIM, q2, zero)
        qm_ref[2 * p + 1] = jnp.where(lane >= HEAD_DIM, q2, zero)
    acc_ref[...] = jnp.zeros_like(acc_ref)
    carry_ref[...] = jnp.zeros_like(carry_ref)
    heads = range(n_head)
    cols = [slice((hh // 2) * LANES, (hh // 2 + 1) * LANES) for hh in heads]

    def tile(j, masked, r0, r1, kw, merge=False):
        rows = r1 - r0
        k0 = pl.multiple_of(j * tk, tk)
        kj = k_ref[0, pl.ds(k0, kw), :]
        vj = v_ref[0, pl.ds(k0, kw), :]
        if masked:
            q_pos = q_start + r0 + lax.broadcasted_iota(jnp.int32, (rows, kw), 0)
            k_pos = k0 + lax.broadcasted_iota(jnp.int32, (rows, kw), 1)
            valid = k_pos < q_pos
        ws = [lax.dot_general(qm_ref[hh, r0:r1], kj[:, cols[hh]], (((1,), (1,)), ((), ())),
                              preferred_element_type=F32) for hh in heads]
        lfs = []
        for hh in heads:
            w = ws[hh]
            lf = jnp.minimum(w, 0.0) - jnp.log2(1.0 + jnp.exp2(jnp.minimum(w, -w)))
            if masked:
                lf = jnp.where(valid, lf, 0.0)
            lfs.append(lf.astype(BF16))
        css = [jnp.dot(lfs[hh], tri_ref[0:kw, 0:kw], preferred_element_type=F32) for hh in heads]
        ats = []
        worst = None
        for hh in heads:
            t1 = css[hh] + jnp.concatenate([carry_ref[hh, r0:r1]] * (kw // LANES), axis=1)
            a = jnp.exp2(t1 - ws[hh])
            if masked:
                a = jnp.where(valid, a, 0.0)
            ats.append(a.astype(BF16))
            total = t1[:, 0:1]
            carry_ref[hh, r0:r1] = jnp.broadcast_to(total, (rows, LANES))
            worst = total if worst is None else jnp.maximum(worst, total)
        for hh in heads:
            acc_ref[hh, r0:r1] += jnp.dot(ats[hh], vj[:, cols[hh]], preferred_element_type=F32)
        alive = lambda lo, hi: (jnp.max(worst[lo - r0:hi - r0]) >= DEAD_LOG2).astype(jnp.int32)
        if r0 < top:
            low = alive(r0, min(r1, top))
            live_ref[0] = jnp.maximum(live_ref[0], low) if merge else low
        if not merge:
            live_ref[1] = jnp.int32(0)
        if r1 > top:
            live_ref[1] = alive(max(r0, top), r1)

    half = tq // 2
    if tq == tk and half % LANES == 0 and half <= top:
        tile(j_diag, True, 0, half, half)
        tile(j_diag, True, half, tq, tk, merge=True)
    else:
        tile(j_diag, True, 0, tq, tk)

    def body(c):
        j, _, live_bottom = c
        if top < tq:
            @pl.when(live_bottom > 0)
            def _():
                tile(j, False, 0, half, tk)
                tile(j, False, half, tq, tk, merge=True)

            @pl.when(live_bottom == 0)
            def _():
                tile(j, False, 0, half, tk)
                tile(j, False, half, top, tk, merge=True)
        else:
            tile(j, False, 0, tq, tk)
        return j - 1, live_ref[0], live_ref[1]

    lax.while_loop(lambda c: jnp.logical_and(c[0] >= 0, c[1] + c[2] > 0), body,
                   (j_diag - 1, live_ref[0], live_ref[1]))
    gated = []
    for p in range(n_pair):
        att = jnp.where(lane < HEAD_DIM, acc_ref[2 * p], acc_ref[2 * p + 1])
        gated.append((att * ga_ref[0, :, p * LANES:(p + 1) * LANES].astype(F32)).astype(BF16))
    ma = jnp.concatenate(gated, axis=1)
    conv = jnp.dot(cn_ref[0], wpw_ref[...], preferred_element_type=F32) + bpw_ref[...]
    mc = (conv * gc_ref[0].astype(F32)).astype(BF16)
    mix = (jnp.dot(ma, wout_ref[0:D_ATT], preferred_element_type=F32)
           + jnp.dot(mc, wout_ref[D_ATT:], preferred_element_type=F32))
    xn = x_ref[0] + mix
    r = lax.rsqrt(jnp.mean(xn * xn, axis=-1, keepdims=True) + EPS)
    y_ref[0] = xn * r * fnw_ref[...]


def _tri(tk):
    j = jnp.arange(tk)[:, None]
    s = jnp.arange(tk)[None, :]
    return (j >= s).astype(BF16)


def _sb_attn_tail(q, k, v, ga, x, cn, gc, w_pw_bf, b_pw, w_out_bf, fn_w, tq, tk, q_pos0):
    b, sq, _ = q.shape
    lk = k.shape[1]
    assert tk % tq == 0 and q_pos0 % tq == 0 and lk % tk == 0 and q_pos0 + sq <= lk
    top = 3 * tq // 4 if tq % 64 == 0 else tq
    row = lambda bi, i: (bi, i, 0)
    const = lambda bi, i: (0, 0)
    half = pl.BlockSpec((1, tq, D_ATT), row)
    full = pl.BlockSpec((1, tq, D_MODEL), row)
    kspec = pl.BlockSpec((1, lk, D_ATT), lambda bi, i: (bi, 0, 0))
    return pl.pallas_call(
        functools.partial(_sb_attn_kernel, tq=tq, tk=tk, q_pos0=q_pos0, top=top),
        grid=(b, sq // tq),
        in_specs=[half, kspec, kspec, half,
                  pl.BlockSpec((tk, tk), const),
                  full, half, half,
                  pl.BlockSpec((D_CONV, D_CONV), const),
                  pl.BlockSpec((1, D_CONV), const),
                  pl.BlockSpec((D_MODEL, D_MODEL), const),
                  pl.BlockSpec((1, D_MODEL), const)],
        out_specs=full,
        out_shape=jax.ShapeDtypeStruct((b, sq, D_MODEL), F32),
        scratch_shapes=[pltpu.VMEM((N_HEADS, tq, LANES), BF16),
                        pltpu.VMEM((N_HEADS, tq, LANES), F32),
                        pltpu.VMEM((N_HEADS, tq, LANES), F32),
                        pltpu.SMEM((2,), jnp.int32)],
        compiler_params=pltpu.CompilerParams(
            dimension_semantics=("arbitrary", "arbitrary"), vmem_limit_bytes=VMEM_LIMIT),
        name="sb_attn_tail",
    )(q, k, v, ga, _tri(tk), x, cn, gc, w_pw_bf, b_pw, w_out_bf, fn_w)


def _pad_rows(t, n):
    return jnp.pad(t, ((0, 0), (0, n - t.shape[1]), (0, 0)))


def _from_heads_bf(t):
    b, h, s, d = t.shape
    return t.astype(BF16).transpose(0, 2, 1, 3).reshape(b, s, h * d)


def kernel(x_prompt, x_sample, cache_k, cache_v, state_conv, norm_w, w_in, dw_w, dw_b, cn_w, cn_b,
           w_pw, b_pw, w_out, final_norm_w):
    depth = w_in.shape[0]
    assert depth == 1
    l = 0
    b, s, _ = x_prompt.shape
    db, n_new, _ = x_sample.shape
    past = cache_k.shape[3]

    w_in_bf = w_in[l].astype(BF16)
    conv_w = (jnp.broadcast_to(dw_w[l][:, None, :], (CONV_W, SUBLANES, D_CONV)),
              jnp.broadcast_to(dw_b[l][None], (SUBLANES, D_CONV)), cn_w[l][None], cn_b[l][None])
    tail_w = (w_pw[l].astype(BF16), b_pw[l][None], w_out[l].astype(BF16), final_norm_w[None])
    nw = norm_w[l][None]

    tq_p = min(256, s)
    hist0 = jnp.zeros((b, HIST, D_CONV), F32)
    q, k, v, ga, cn, gc, kp, vp, cp = _in_proj(x_prompt, nw, w_in_bf, hist0, *conv_w, ts=min(512, s))
    y_p = _sb_attn_tail(q, k, v, ga, x_prompt, cn, gc, *tail_w, tq=tq_p, tk=tq_p, q_pos0=0)

    tk_s = 256
    hist_s = jnp.pad(state_conv[l], ((0, 0), (HIST_OFF, 0), (0, 0)))
    q, k, v, ga, cn, gc, ks, vs, cs = _in_proj(x_sample, nw, w_in_bf, hist_s, *conv_w, ts=n_new)
    lk = -(-(past + n_new) // tk_s) * tk_s
    k_all = _pad_rows(jnp.concatenate([_from_heads_bf(cache_k[l]), k], axis=1), lk)
    v_all = _pad_rows(jnp.concatenate([_from_heads_bf(cache_v[l]), v], axis=1), lk)
    y_s = _sb_attn_tail(q, k_all, v_all, ga, x_sample, cn, gc, *tail_w, tq=n_new, tk=tk_s, q_pos0=past)

    return (y_p, y_s, kp, vp, cp, ks, vs, cs)
```
